```python
import math
import jax, jax.numpy as jnp
from jax import lax
import numpy as np

D_MODEL = 2048
BATCH = 1
SEQ = 16384
DEPTH = 1

CHUNK = 64
Q_BLOCK = 128
N_HEADS = 8
HEAD_DIM = D_MODEL // (2 * N_HEADS)
ATTN_WIDTH = 2 * N_HEADS * HEAD_DIM
CONV_WIDTH = D_MODEL
CONV_KERNEL = 31
D_FF = -(-(8 * D_MODEL) // (3 * 256)) * 256
NORM_EPS = 1e-5
PROJ_SIZES = (ATTN_WIDTH, ATTN_WIDTH, ATTN_WIDTH, CONV_WIDTH, CONV_WIDTH, D_MODEL, D_MODEL)
PROJ_SPLITS = tuple(int(s) for s in np.cumsum(PROJ_SIZES)[:-1])
PROJ_WIDTH = int(sum(PROJ_SIZES))

kernel_name = "hybrid_diffattn_conformer_gated_block"


def _lambda_init(layer_idx):
    return 0.8 - 0.6 * math.exp(-0.3 * layer_idx)


def _rmsnorm(x, g):
    xf = x.astype(jnp.float32)
    y = xf * lax.rsqrt(jnp.mean(xf * xf, axis=-1, keepdims=True) + NORM_EPS)
    return (y * g.astype(jnp.float32)).astype(x.dtype)


def _layernorm(x, g, b):
    xf = x.astype(jnp.float32)
    mu = jnp.mean(xf, axis=-1, keepdims=True)
    xc = xf - mu
    var = jnp.mean(xc * xc, axis=-1, keepdims=True)
    y = xc * lax.rsqrt(var + NORM_EPS) * g.astype(jnp.float32) + b.astype(jnp.float32)
    return y.astype(x.dtype)


def _alibi_slopes(n_heads):
    return jnp.exp2(-8.0 * jnp.arange(1, n_heads + 1, dtype=jnp.float32) / n_heads)


def _diff_attention(q, k, v, lam):
    B, S = q.shape[0], q.shape[1]
    n_blocks = S // Q_BLOCK
    slopes = _alibi_slopes(N_HEADS)[:, None, None, None]
    key_pos = jnp.arange(S, dtype=jnp.int32)
    scale = HEAD_DIM ** -0.5
    kf = k.astype(jnp.float32)
    vf = v.astype(jnp.float32)

    def one_block(blk):
        start = blk * Q_BLOCK
        qb = lax.dynamic_slice_in_dim(q, start, Q_BLOCK, axis=1).astype(jnp.float32)
        q_pos = start + jnp.arange(Q_BLOCK, dtype=jnp.int32)
        s = jnp.einsum('bqhjd,bkhjd->bhjqk', qb, kf) * scale
        dist = jnp.abs(q_pos[:, None] - key_pos[None, :]).astype(jnp.float32)
        allowed = (key_pos[None, :] // CHUNK) <= (q_pos[:, None] // CHUNK)
        s = jnp.where(allowed, s - slopes * dist, -jnp.inf)
        p = jax.nn.softmax(s, axis=-1)
        w = p[:, :, 0] - lam * p[:, :, 1]
        return jnp.einsum('bhqk,bkhe->bqhe', w, vf)

    o = lax.map(one_block, jnp.arange(n_blocks, dtype=jnp.int32))
    return jnp.moveaxis(o, 0, 1).reshape(B, S, N_HEADS, 2 * HEAD_DIM)


def _conformer_conv(a, b, dw_w, dw_b, ln_g, ln_b, w_out, b_out):
    y = a * jax.nn.sigmoid(b)
    y = lax.conv_general_dilated(
        y, dw_w[:, None, :].astype(y.dtype), window_strides=(1,),
        padding=[(CONV_KERNEL - 1, 0)],
        dimension_numbers=('NWC', 'WIO', 'NWC'),
        feature_group_count=CONV_WIDTH) + dw_b
    y = jax.nn.silu(_layernorm(y, ln_g, ln_b))
    return y @ w_out + b_out


def setup_inputs(seed: int = 0) -> dict:
    key = jax.random.key(seed)
    ks = jax.random.split(key, 24)
    f32 = jnp.float32
    L = DEPTH

    def nrm(k, shape, scale):
        return jax.random.normal(k, shape, f32) * scale

    return {
        "x": jax.random.normal(ks[0], (BATCH, SEQ, D_MODEL), f32),
        "norm_mix_g": 1.0 + nrm(ks[1], (L, D_MODEL), 0.02),
        "w_in": nrm(ks[2], (L, D_MODEL, PROJ_WIDTH), D_MODEL ** -0.5),
        "lambda_q1": nrm(ks[3], (L, HEAD_DIM), 0.1),
        "lambda_k1": nrm(ks[4], (L, HEAD_DIM), 0.1),
        "lambda_q2": nrm(ks[5], (L, HEAD_DIM), 0.1),
        "lambda_k2": nrm(ks[6], (L, HEAD_DIM), 0.1),
        "subln_g": 1.0 + nrm(ks[7], (L, 2 * HEAD_DIM), 0.02),
        "w_attn_out": nrm(ks[8], (L, ATTN_WIDTH, D_MODEL), ATTN_WIDTH ** -0.5),
        "dw_w": nrm(ks[9], (L, CONV_KERNEL, CONV_WIDTH), CONV_KERNEL ** -0.5),
        "dw_b": nrm(ks[10], (L, CONV_WIDTH), 0.02),
        "conv_ln_g": 1.0 + nrm(ks[11], (L, CONV_WIDTH), 0.02),
        "conv_ln_b": nrm(ks[12], (L, CONV_WIDTH), 0.02),
        "w_conv_out": nrm(ks[13], (L, CONV_WIDTH, D_MODEL), CONV_WIDTH ** -0.5),
        "b_conv_out": nrm(ks[14], (L, D_MODEL), 0.02),
        "w_mix_out": nrm(ks[15], (L, D_MODEL, D_MODEL), D_MODEL ** -0.5),
        "norm_ffn_g": 1.0 + nrm(ks[16], (L, D_MODEL), 0.02),
        "w_ffn_in": nrm(ks[17], (L, D_MODEL, 2 * D_FF), D_MODEL ** -0.5),
        "w_ffn_out": nrm(ks[18], (L, D_FF, D_MODEL), D_FF ** -0.5),
        "norm_final_g": 1.0 + nrm(ks[19], (D_MODEL,), 0.02),
    }


def reference(x, norm_mix_g, w_in, lambda_q1, lambda_k1, lambda_q2, lambda_k2, subln_g,
              w_attn_out, dw_w, dw_b, conv_ln_g, conv_ln_b, w_conv_out, b_conv_out,
              w_mix_out, norm_ffn_g, w_ffn_in, w_ffn_out, norm_final_g):
    B, S, _ = x.shape
    h = x
    for l in range(DEPTH):
        lam_init = _lambda_init(l)
        u = _rmsnorm(h, norm_mix_g[l])
        proj = u @ w_in[l]
        q, k, v, ca, cb, ga, gc = jnp.split(proj, PROJ_SPLITS, axis=-1)

        lam = (jnp.exp(jnp.sum(lambda_q1[l].astype(jnp.float32) * lambda_k1[l].astype(jnp.float32)))
               - jnp.exp(jnp.sum(lambda_q2[l].astype(jnp.float32) * lambda_k2[l].astype(jnp.float32)))
               + lam_init)
        o = _diff_attention(q.reshape(B, S, N_HEADS, 2, HEAD_DIM),
                            k.reshape(B, S, N_HEADS, 2, HEAD_DIM),
                            v.reshape(B, S, N_HEADS, 2 * HEAD_DIM), lam)
        o = _rmsnorm(o, subln_g[l]) * (1.0 - lam_init)
        attn_branch = o.reshape(B, S, ATTN_WIDTH).astype(h.dtype) @ w_attn_out[l]

        conv_branch = _conformer_conv(ca, cb, dw_w[l], dw_b[l], conv_ln_g[l], conv_ln_b[l],
                                      w_conv_out[l], b_conv_out[l])

        mixed = jax.nn.sigmoid(ga) * attn_branch + jax.nn.sigmoid(gc) * conv_branch
        h = h + mixed @ w_mix_out[l]

        f = _rmsnorm(h, norm_ffn_g[l])
        g_ff, up_ff = jnp.split(f @ w_ffn_in[l], 2, axis=-1)
        h = h + (jax.nn.silu(g_ff) * up_ff) @ w_ffn_out[l]
    return _rmsnorm(h, norm_final_g)
```

```python
import functools
import math

import jax
import jax.numpy as jnp
from jax import lax
from jax.experimental import pallas as pl
from jax.experimental.pallas import tpu as pltpu

F32 = jnp.float32
BF16 = jnp.bfloat16

N_HEADS = 8
HEAD_DIM = 128
HEAD_WIDTH = 2 * HEAD_DIM
CHUNK = 64
CONV_KERNEL = 31
CONV_HALO = 32
NORM_EPS = 1e-5
LAMBDA_INIT = 0.8 - 0.6 * math.exp(-0.3 * 0)
LOG2E = math.log2(math.e)
LANES = 128
VMEM_LIMIT = 56 * 1024 * 1024
NEG_BIG = -1e30


def _params(*sem):
    return pltpu.CompilerParams(dimension_semantics=sem, vmem_limit_bytes=VMEM_LIMIT)


def _rms_rows(x, g):
    return x * lax.rsqrt(jnp.mean(x * x, axis=-1, keepdims=True) + NORM_EPS) * g


def _sigmoid(x):
    return 1.0 / (1.0 + jnp.exp(-x))


NORM_ROWS = 256


def _norm_into(x_ref, g_ref, u_ref):
    @pl.when(pl.program_id(1) == 0)
    def _():
        def body(r, c):
            rows = pl.ds(pl.multiple_of(r * NORM_ROWS, NORM_ROWS), NORM_ROWS)
            u_ref[rows, :] = _rms_rows(x_ref[rows, :], g_ref[...]).astype(BF16)
            return c

        lax.fori_loop(0, x_ref.shape[0] // NORM_ROWS, body, 0)


def _proj_scaled_kernel(x_ref, g_ref, w_ref, cs_ref, o_ref, u_ref):
    _norm_into(x_ref, g_ref, u_ref)
    r = jnp.dot(u_ref[...], w_ref[...], preferred_element_type=F32)
    o_ref[...] = (r * cs_ref[...]).astype(o_ref.dtype)


def _proj_glu_kernel(x_ref, g_ref, wa_ref, wb_ref, o_ref, u_ref):
    _norm_into(x_ref, g_ref, u_ref)
    u = u_ref[...]
    a = jnp.dot(u, wa_ref[...], preferred_element_type=F32)
    b = jnp.dot(u, wb_ref[...], preferred_element_type=F32)
    o_ref[...] = (a * _sigmoid(b)).astype(o_ref.dtype)


def _proj_sigmoid_kernel(x_ref, g_ref, w_ref, o_ref, u_ref):
    _norm_into(x_ref, g_ref, u_ref)
    r = jnp.dot(u_ref[...], w_ref[...], preferred_element_type=F32)
    o_ref[...] = _sigmoid(r).astype(o_ref.dtype)


def _norm_proj(kern, x, g, weights, col_blocks, n_out, tm, tn, extra=()):
    S, D = x.shape
    in_specs = [pl.BlockSpec((tm, D), lambda i, j: (i, 0)),
                pl.BlockSpec((1, D), lambda i, j: (0, 0))]
    args = [x, g]
    for w, off in weights:
        in_specs.append(pl.BlockSpec((D, tn), lambda i, j, off=off: (0, off + j)))
        args.append(w)
    for e in extra:
        in_specs.append(pl.BlockSpec((1, tn), lambda i, j: (0, j)))
        args.append(e)
    return pl.pallas_call(
        kern,
        grid=(S // tm, col_blocks),
        in_specs=in_specs,
        out_specs=pl.BlockSpec((tm, tn), lambda i, j: (i, j)),
        out_shape=jax.ShapeDtypeStruct((S, n_out), BF16),
        scratch_shapes=[pltpu.VMEM((tm, D), BF16)],
        compiler_params=_params("arbitrary", "arbitrary"),
    )(*args)


def _attn_kernel(slope_ref, q_ref, k_ref, v_ref, b0_ref, lq1_ref, lk1_ref, lq2_ref, lk2_ref, g_ref,
                 o_ref, m_ref, l_ref, acc_ref, *, tq, tk):
    h = pl.program_id(0)
    i = pl.program_id(1)
    slope = slope_ref[h]
    shift = slope * tk
    reps = tk // LANES

    m_ref[...] = jnp.full(m_ref.shape, NEG_BIG, F32)
    l_ref[...] = jnp.zeros(l_ref.shape, F32)
    acc_ref[...] = jnp.zeros(acc_ref.shape, F32)

    q = (q_ref[:, :HEAD_DIM], q_ref[:, HEAD_DIM:])

    def block(k_blk, v_blk, bias):
        ps, alphas = [], []
        for j in range(2):
            s = lax.dot_general(q[j], k_blk[:, j * HEAD_DIM:(j + 1) * HEAD_DIM],
                                (((1,), (1,)), ((), ())), preferred_element_type=F32) + bias
            m_prev = m_ref[j] - shift
            m_new = jnp.maximum(m_prev, jnp.max(s, axis=1, keepdims=True))
            alpha = jnp.exp2(m_prev - m_new)
            p = jnp.exp2(s - jnp.concatenate([m_new] * reps, axis=1))
            l_ref[j] = alpha * l_ref[j] + jnp.sum(p, axis=1, keepdims=True)
            m_ref[j] = m_new
            ps.append(p.astype(BF16))
            alphas.append(jnp.concatenate([alpha] * (HEAD_WIDTH // LANES), axis=1))
        pv = jnp.dot(jnp.concatenate(ps, axis=0), v_blk, preferred_element_type=F32)
        acc_ref[...] = acc_ref[...] * jnp.concatenate(alphas, axis=0) + pv

    col_bias = slope * lax.broadcasted_iota(jnp.int32, (1, tk), 1).astype(F32)

    def full_block(kb, c):
        rows = pl.ds(pl.multiple_of(kb * tk, tk), tk)
        block(k_ref[rows, :], v_ref[rows, :], col_bias)
        return c

    lax.fori_loop(0, i, full_block, 0)

    rows = pl.ds(pl.multiple_of(i * tq, tq), tk)
    block(k_ref[rows, :], v_ref[rows, :], slope * b0_ref[...])

    lam = (jnp.exp(jnp.sum(lq1_ref[...] * lk1_ref[...], axis=1, keepdims=True))
           - jnp.exp(jnp.sum(lq2_ref[...] * lk2_ref[...], axis=1, keepdims=True)) + LAMBDA_INIT)
    o1 = acc_ref[:tq, :] / jnp.concatenate([l_ref[0]] * (HEAD_WIDTH // LANES), axis=1)
    o2 = acc_ref[tq:, :] / jnp.concatenate([l_ref[1]] * (HEAD_WIDTH // LANES), axis=1)
    o = o1 - lam * o2
    o_ref[...] = (_rms_rows(o, g_ref[...]) * (1.0 - LAMBDA_INIT)).astype(o_ref.dtype)


def _diag_bias(t):
    r = lax.broadcasted_iota(jnp.int32, (t, t), 0)
    c = lax.broadcasted_iota(jnp.int32, (t, t), 1)
    allowed = (c // CHUNK) <= (r // CHUNK)
    return jnp.where(allowed, jnp.minimum(c, 2 * r - c).astype(F32), -jnp.inf)


def _attention(qkv, lq1, lk1, lq2, lk2, subln_g, t):
    S = qkv.shape[0]
    slopes = jnp.exp2(-8.0 * jnp.arange(1, N_HEADS + 1, dtype=F32) / N_HEADS) * LOG2E
    vec = pl.BlockSpec((1, HEAD_DIM), lambda h, i: (0, 0))
    return pl.pallas_call(
        functools.partial(_attn_kernel, tq=t, tk=t),
        grid=(N_HEADS, S // t),
        in_specs=[pl.BlockSpec(memory_space=pltpu.SMEM),
                  pl.BlockSpec((t, HEAD_WIDTH), lambda h, i: (i, h)),
                  pl.BlockSpec((S, HEAD_WIDTH), lambda h, i: (0, N_HEADS + h)),
                  pl.BlockSpec((S, HEAD_WIDTH), lambda h, i: (0, 2 * N_HEADS + h)),
                  pl.BlockSpec((t, t), lambda h, i: (0, 0)),
                  vec, vec, vec, vec,
                  pl.BlockSpec((1, HEAD_WIDTH), lambda h, i: (0, 0))],
        out_specs=pl.BlockSpec((t, HEAD_WIDTH), lambda h, i: (i, h)),
        out_shape=jax.ShapeDtypeStruct((S, N_HEADS * HEAD_WIDTH), BF16),
        scratch_shapes=[pltpu.VMEM((2, t, LANES), F32),
                        pltpu.VMEM((2, t, LANES), F32),
                        pltpu.VMEM((2 * t, HEAD_WIDTH), F32)],
        compiler_params=_params("arbitrary", "arbitrary"),
    )(slopes, qkv, qkv, qkv, _diag_bias(t), lq1, lk1, lq2, lk2, subln_g)


CONV_ROWS = 64
CONV_COLS = 512


def _conv_kernel(yp_ref, y_ref, w_ref, b_ref, lg_ref, lb_ref, z_ref, ext_ref, cv_ref):
    tm, C = y_ref.shape
    first = pl.program_id(0) == 0
    ext_ref[:CONV_HALO, :] = jnp.where(first, 0.0, yp_ref[...].astype(F32))
    ext_ref[CONV_HALO:, :] = y_ref[...].astype(F32)
    lead = CONV_HALO - (CONV_KERNEL - 1)
    for c0 in range(0, C, CONV_COLS):
        cols = slice(c0, c0 + CONV_COLS)
        for r0 in range(0, tm, CONV_ROWS):
            acc = jnp.zeros((CONV_ROWS, CONV_COLS), F32) + b_ref[:, cols]
            for j in range(CONV_KERNEL):
                acc = acc + ext_ref[r0 + lead + j:r0 + lead + j + CONV_ROWS, cols] * w_ref[j:j + 1, cols]
            cv_ref[r0:r0 + CONV_ROWS, cols] = acc
    cv = cv_ref[...]
    mu = jnp.mean(cv, axis=-1, keepdims=True)
    xc = cv - mu
    var = jnp.mean(xc * xc, axis=-1, keepdims=True)
    zn = xc * lax.rsqrt(var + NORM_EPS) * lg_ref[...] + lb_ref[...]
    z_ref[...] = (zn * _sigmoid(zn)).astype(z_ref.dtype)


def _conv_branch(y, dw_w, dw_b, ln_g, ln_b, tm):
    S, C = y.shape
    halo_blocks = tm // CONV_HALO
    row = pl.BlockSpec((1, C), lambda i: (0, 0))
    return pl.pallas_call(
        _conv_kernel,
        grid=(S // tm,),
        in_specs=[pl.BlockSpec((CONV_HALO, C), lambda i: (jnp.maximum(i * halo_blocks - 1, 0), 0)),
                  pl.BlockSpec((tm, C), lambda i: (i, 0)),
                  pl.BlockSpec((CONV_KERNEL, C), lambda i: (0, 0)),
                  row, row, row],
        out_specs=pl.BlockSpec((tm, C), lambda i: (i, 0)),
        out_shape=jax.ShapeDtypeStruct((S, C), BF16),
        scratch_shapes=[pltpu.VMEM((tm + CONV_HALO, C), F32),
                        pltpu.VMEM((tm, C), F32)],
        compiler_params=_params("arbitrary"),
    )(y, y, dw_w, dw_b, ln_g, ln_b)


def _merge_kernel(o_ref, z_ref, wa_ref, wc_ref, bc_ref, ga_ref, gc_ref, out_ref):
    a = jnp.dot(o_ref[...], wa_ref[...], preferred_element_type=F32)
    c = jnp.dot(z_ref[...], wc_ref[...], preferred_element_type=F32) + bc_ref[...]
    out_ref[...] = (ga_ref[...].astype(F32) * a + gc_ref[...].astype(F32) * c).astype(out_ref.dtype)


def _merge(o, z, wa, wc, bc, gates, tm, tn):
    S, D = o.shape
    nb = D // tn
    return pl.pallas_call(
        _merge_kernel,
        grid=(S // tm, nb),
        in_specs=[pl.BlockSpec((tm, D), lambda i, j: (i, 0)),
                  pl.BlockSpec((tm, D), lambda i, j: (i, 0)),
                  pl.BlockSpec((D, tn), lambda i, j: (0, j)),
                  pl.BlockSpec((D, tn), lambda i, j: (0, j)),
                  pl.BlockSpec((1, tn), lambda i, j: (0, j)),
                  pl.BlockSpec((tm, tn), lambda i, j: (i, j)),
                  pl.BlockSpec((tm, tn), lambda i, j: (i, nb + j))],
        out_specs=pl.BlockSpec((tm, tn), lambda i, j: (i, j)),
        out_shape=jax.ShapeDtypeStruct((S, D), BF16),
        compiler_params=_params("arbitrary", "arbitrary"),
    )(o, z, wa, wc, bc, gates, gates)


def _mixout_kernel(m_ref, w_ref, x_ref, g_ref, h_ref, f_ref):
    h = x_ref[...] + jnp.dot(m_ref[...], w_ref[...], preferred_element_type=F32)
    h_ref[...] = h
    f_ref[...] = _rms_rows(h, g_ref[...]).astype(f_ref.dtype)


def _mixout(mixed, w, x, g, tm):
    S, D = x.shape
    tile = pl.BlockSpec((tm, D), lambda i: (i, 0))
    return pl.pallas_call(
        _mixout_kernel,
        grid=(S // tm,),
        in_specs=[tile,
                  pl.BlockSpec((D, D), lambda i: (0, 0)),
                  tile,
                  pl.BlockSpec((1, D), lambda i: (0, 0))],
        out_specs=[tile, tile],
        out_shape=[jax.ShapeDtypeStruct((S, D), F32), jax.ShapeDtypeStruct((S, D), BF16)],
        compiler_params=_params("arbitrary"),
    )(mixed, w, x, g)


def _ffn_in_kernel(f_ref, wg_ref, wu_ref, a_ref):
    f = f_ref[...]
    g = jnp.dot(f, wg_ref[...], preferred_element_type=F32)
    u = jnp.dot(f, wu_ref[...], preferred_element_type=F32)
    a_ref[...] = (g * _sigmoid(g) * u).astype(a_ref.dtype)


def _ffn_in(f, w, tm, tn):
    S, D = f.shape
    d_ff = w.shape[1] // 2
    nb = d_ff // tn
    return pl.pallas_call(
        _ffn_in_kernel,
        grid=(S // tm, nb),
        in_specs=[pl.BlockSpec((tm, D), lambda i, j: (i, 0)),
                  pl.BlockSpec((D, tn), lambda i, j: (0, j)),
                  pl.BlockSpec((D, tn), lambda i, j: (0, nb + j))],
        out_specs=pl.BlockSpec((tm, tn), lambda i, j: (i, j)),
        out_shape=jax.ShapeDtypeStruct((S, d_ff), BF16),
        compiler_params=_params("arbitrary", "arbitrary"),
    )(f, w, w)


def _ffn_out_kernel(a_ref, w_ref, h_ref, g_ref, out_ref):
    k = pl.program_id(1)

    @pl.when(k == 0)
    def _():
        out_ref[...] = h_ref[...]

    out_ref[...] += jnp.dot(a_ref[...], w_ref[...], preferred_element_type=F32)

    @pl.when(k == pl.num_programs(1) - 1)
    def _():
        out_ref[...] = _rms_rows(out_ref[...], g_ref[...])


def _ffn_out(a, w, h, g, tm, tk):
    S, D = h.shape
    d_ff = a.shape[1]
    return pl.pallas_call(
        _ffn_out_kernel,
        grid=(S // tm, d_ff // tk),
        in_specs=[pl.BlockSpec((tm, tk), lambda i, k: (i, k)),
                  pl.BlockSpec((tk, D), lambda i, k: (k, 0)),
                  pl.BlockSpec((tm, D), lambda i, k: (i, 0)),
                  pl.BlockSpec((1, D), lambda i, k: (0, 0))],
        out_specs=pl.BlockSpec((tm, D), lambda i, k: (i, 0)),
        out_shape=jax.ShapeDtypeStruct((S, D), F32),
        compiler_params=_params("arbitrary", "arbitrary"),
    )(a, w, h, g)


def _tiles(S, d_ff):
    return dict(
        proj_rows=min(S, 1024), proj_cols=1024, glu_cols=512,
        attn=min(S, 512),
        conv_rows=min(S, 256),
        merge_rows=min(S, 1024), merge_cols=1024,
        mix_rows=min(S, 512),
        ffn_rows=min(S, 1024), ffn_cols=512,
        out_rows=min(S, 512), out_k=d_ff // 4,
    )


def kernel(x, norm_mix_g, w_in, lambda_q1, lambda_k1, lambda_q2, lambda_k2, subln_g, w_attn_out, dw_w, dw_b,
           conv_ln_g, conv_ln_b, w_conv_out, b_conv_out, w_mix_out, norm_ffn_g, w_ffn_in, w_ffn_out, norm_final_g):
    B, S, D = x.shape
    depth = w_in.shape[0]
    assert B == 1 and depth == 1 and D == N_HEADS * HEAD_WIDTH
    d_ff = w_ffn_out.shape[1]
    t = _tiles(S, d_ff)
    row = lambda v: v.reshape(1, -1).astype(F32)

    h0 = x.reshape(S, D)
    w_in_b = w_in[0].astype(BF16)

    q_scale = HEAD_DIM ** -0.5 * LOG2E
    col_scale = jnp.concatenate([jnp.full((1, D), q_scale, F32), jnp.ones((1, 2 * D), F32)], axis=1)
    pc = t["proj_cols"]
    qkv = _norm_proj(_proj_scaled_kernel, h0, row(norm_mix_g[0]), [(w_in_b, 0)], 3 * D // pc, 3 * D,
                     t["proj_rows"], pc, extra=(col_scale,))
    gc = t["glu_cols"]
    y = _norm_proj(_proj_glu_kernel, h0, row(norm_mix_g[0]), [(w_in_b, 3 * D // gc), (w_in_b, 4 * D // gc)],
                   D // gc, D, t["proj_rows"], gc)
    gates = _norm_proj(_proj_sigmoid_kernel, h0, row(norm_mix_g[0]), [(w_in_b, 5 * D // pc)], 2 * D // pc, 2 * D,
                       t["proj_rows"], pc)

    o = _attention(qkv, row(lambda_q1[0]), row(lambda_k1[0]), row(lambda_q2[0]), row(lambda_k2[0]),
                   row(subln_g[0]), t["attn"])
    z = _conv_branch(y, dw_w[0].astype(F32), row(dw_b[0]), row(conv_ln_g[0]), row(conv_ln_b[0]), t["conv_rows"])

    mixed = _merge(o, z, w_attn_out[0].astype(BF16), w_conv_out[0].astype(BF16), row(b_conv_out[0]), gates,
                   t["merge_rows"], t["merge_cols"])
    h1, f = _mixout(mixed, w_mix_out[0].astype(BF16), h0, row(norm_ffn_g[0]), t["mix_rows"])

    a = _ffn_in(f, w_ffn_in[0].astype(BF16), t["ffn_rows"], t["ffn_cols"])
    out = _ffn_out(a, w_ffn_out[0].astype(BF16), h1, row(norm_final_g), t["out_rows"], t["out_k"])
    return out.reshape(B, S, D)
```

```python
import functools
import math

import jax
import jax.numpy as jnp
import numpy as np
from jax import lax
from jax.experimental import pallas as pl
from jax.experimental.pallas import tpu as pltpu

F32 = jnp.float32
BF16 = jnp.bfloat16

N_HEADS = 8
HEAD_DIM = 128
HEAD_WIDTH = 2 * HEAD_DIM
CHUNK = 64
CONV_KERNEL = 31
CONV_HALO = 32
NORM_EPS = 1e-5
LAMBDA_INIT = 0.8 - 0.6 * math.exp(-0.3 * 0)
LOG2E = math.log2(math.e)
LANES = 128
VMEM_LIMIT = 56 * 1024 * 1024
NEG_BIG = -1e30


def _params(*sem):
    return pltpu.CompilerParams(dimension_semantics=sem, vmem_limit_bytes=VMEM_LIMIT)


def _rms_rows(x, g):
    return x * lax.rsqrt(jnp.mean(x * x, axis=-1, keepdims=True) + NORM_EPS) * g


def _sigmoid(x):
    return 1.0 / (1.0 + jnp.exp(-x))


NORM_ROWS = 256


def _norm_into(x_ref, g_ref, u_ref):
    @pl.when(pl.program_id(1) == 0)
    def _():
        def body(r, c):
            rows = pl.ds(pl.multiple_of(r * NORM_ROWS, NORM_ROWS), NORM_ROWS)
            u_ref[rows, :] = _rms_rows(x_ref[rows, :], g_ref[...]).astype(BF16)
            return c

        lax.fori_loop(0, x_ref.shape[0] // NORM_ROWS, body, 0)


def _proj_scaled_kernel(x_ref, g_ref, w_ref, cs_ref, o_ref, u_ref):
    _norm_into(x_ref, g_ref, u_ref)
    r = jnp.dot(u_ref[...], w_ref[...], preferred_element_type=F32)
    o_ref[...] = (r * cs_ref[...]).astype(o_ref.dtype)


def _proj_glu_kernel(x_ref, g_ref, wa_ref, wb_ref, o_ref, u_ref):
    _norm_into(x_ref, g_ref, u_ref)
    u = u_ref[...]
    a = jnp.dot(u, wa_ref[...], preferred_element_type=F32)
    b = jnp.dot(u, wb_ref[...], preferred_element_type=F32)
    o_ref[...] = (a * _sigmoid(b)).astype(o_ref.dtype)


def _proj_sigmoid_kernel(x_ref, g_ref, w_ref, o_ref, u_ref):
    _norm_into(x_ref, g_ref, u_ref)
    r = jnp.dot(u_ref[...], w_ref[...], preferred_element_type=F32)
    o_ref[...] = _sigmoid(r).astype(o_ref.dtype)


def _norm_proj(kern, x, g, weights, col_blocks, n_out, tm, tn, extra=()):
    S, D = x.shape
    in_specs = [pl.BlockSpec((tm, D), lambda i, j: (i, 0)),
                pl.BlockSpec((1, D), lambda i, j: (0, 0))]
    args = [x, g]
    for w, off in weights:
        in_specs.append(pl.BlockSpec((D, tn), lambda i, j, off=off: (0, off + j)))
        args.append(w)
    for e in extra:
        in_specs.append(pl.BlockSpec((1, tn), lambda i, j: (0, j)))
        args.append(e)
    return pl.pallas_call(
        kern,
        grid=(S // tm, col_blocks),
        in_specs=in_specs,
        out_specs=pl.BlockSpec((tm, tn), lambda i, j: (i, j)),
        out_shape=jax.ShapeDtypeStruct((S, n_out), BF16),
        scratch_shapes=[pltpu.VMEM((tm, D), BF16)],
        compiler_params=_params("arbitrary", "arbitrary"),
        name=kern.__name__.strip("_"),
    )(*args)


def _attn_kernel(slope_ref, q_ref, k_ref, v_ref, qaug_ref, kaug_ref, dx_ref, lq1_ref, lk1_ref, lq2_ref, lk2_ref,
                 g_ref, o_ref, m_ref, l_ref, acc_ref, sa_ref, sb_ref, xa_ref, xb_ref, *, t):
    h = pl.program_id(0)
    i = pl.program_id(1)
    slope = slope_ref[h]
    shift = slope * t
    reps = t // LANES

    m_ref[...] = jnp.full(m_ref.shape, NEG_BIG, F32)
    l_ref[...] = jnp.zeros(l_ref.shape, F32)
    acc_ref[...] = jnp.zeros(acc_ref.shape, F32)

    q_extra = jnp.broadcast_to(qaug_ref[...], (t, HEAD_DIM)).astype(BF16)
    q = tuple(jnp.concatenate([q_ref[:, j * HEAD_DIM:(j + 1) * HEAD_DIM], q_extra], axis=1) for j in range(2))

    def block_rows(blk):
        return pl.ds(pl.multiple_of(blk * t, t), t)

    def scores(blk, s_ref, x_ref):
        rows = block_rows(blk)
        for j in range(2):
            k_j = jnp.concatenate([k_ref[rows, j * HEAD_DIM:(j + 1) * HEAD_DIM], kaug_ref[...]], axis=1)
            s = lax.dot_general(q[j], k_j, (((1,), (1,)), ((), ())), preferred_element_type=F32)
            s_ref[j] = s
            lane_max = s[:, :LANES]
            for r in range(1, reps):
                lane_max = jnp.maximum(lane_max, s[:, r * LANES:(r + 1) * LANES])
            x_ref[j] = lane_max

    def softmax_pv(blk, s_ref, x_ref, diagonal):
        ps, alphas = [], []
        for j in range(2):
            if diagonal:
                s = s_ref[j] + slope * dx_ref[...]
                m_cur = jnp.max(s, axis=1, keepdims=True)
            else:
                s = s_ref[j]
                m_cur = jnp.max(x_ref[j], axis=1, keepdims=True)
            m_prev = m_ref[j] - shift
            m_new = jnp.maximum(m_prev, m_cur)
            alpha = jnp.exp2(m_prev - m_new)
            p = jnp.exp2(s - jnp.concatenate([m_new] * reps, axis=1))
            lane_sums = p[:, :LANES]
            for r in range(1, reps):
                lane_sums = lane_sums + p[:, r * LANES:(r + 1) * LANES]
            l_ref[j] = alpha * l_ref[j] + lane_sums
            m_ref[j] = m_new
            ps.append(p.astype(BF16))
            alphas.append(jnp.concatenate([alpha] * (HEAD_WIDTH // LANES), axis=1))
        pv = jnp.dot(jnp.concatenate(ps, axis=0), v_ref[block_rows(blk), :], preferred_element_type=F32)
        acc_ref[...] = acc_ref[...] * jnp.concatenate(alphas, axis=0) + pv

    buf_a = (sa_ref, xa_ref)
    buf_b = (sb_ref, xb_ref)

    def stage(blk, cur, nxt):
        scores(blk + 1, *nxt)
        softmax_pv(blk, *cur, diagonal=False)

    scores(0, *buf_a)

    def stage_pair(u, c):
        stage(2 * u, buf_a, buf_b)
        stage(2 * u + 1, buf_b, buf_a)
        return c

    lax.fori_loop(0, lax.shift_right_logical(i, 1), stage_pair, 0)
    odd = (i & 1) == 1

    @pl.when(odd)
    def _():
        stage(i - 1, buf_a, buf_b)
        softmax_pv(i, *buf_b, diagonal=True)

    @pl.when(jnp.logical_not(odd))
    def _():
        softmax_pv(i, *buf_a, diagonal=True)

    lam = (jnp.exp(jnp.sum(lq1_ref[...] * lk1_ref[...], axis=1, keepdims=True))
           - jnp.exp(jnp.sum(lq2_ref[...] * lk2_ref[...], axis=1, keepdims=True)) + LAMBDA_INIT)
    o1 = acc_ref[:t, :] / jnp.sum(l_ref[0], axis=1, keepdims=True)
    o2 = acc_ref[t:, :] / jnp.sum(l_ref[1], axis=1, keepdims=True)
    o = o1 - lam * o2
    o_ref[...] = (_rms_rows(o, g_ref[...]) * (1.0 - LAMBDA_INIT)).astype(o_ref.dtype)


BIAS_SPLIT = 256
LOG2E_PIECES = 3


def _diag_extra(t):
    r = lax.broadcasted_iota(jnp.int32, (t, t), 0)
    c = lax.broadcasted_iota(jnp.int32, (t, t), 1)
    allowed = (c // CHUNK) <= (r // CHUNK)
    return jnp.where(allowed, (-2 * jnp.maximum(c - r, 0)).astype(F32), -jnp.inf)


def _bias_columns(t):
    pieces, rest = [], LOG2E
    for _ in range(LOG2E_PIECES):
        p = float(np.asarray(rest, np.float32).astype(BF16).astype(np.float32))
        pieces.append(p)
        rest -= p
    q_cols = jnp.zeros((1, HEAD_DIM), F32).at[0, :2 * LOG2E_PIECES].set(jnp.asarray(pieces * 2, F32))
    slopes = jnp.exp2(-(jnp.arange(N_HEADS, dtype=F32) + 1.0) * (8.0 / N_HEADS))
    c = jnp.arange(t, dtype=jnp.int32)
    lo = (c % BIAS_SPLIT).astype(F32)
    hi = (c - c % BIAS_SPLIT).astype(F32)
    k_cols = jnp.concatenate([jnp.tile(lo[:, None], (1, LOG2E_PIECES)), jnp.tile(hi[:, None], (1, LOG2E_PIECES)),
                              jnp.zeros((t, HEAD_DIM - 2 * LOG2E_PIECES), F32)], axis=1)
    k_cols = (slopes[:, None, None] * k_cols[None]).astype(BF16)
    return q_cols, k_cols, slopes * LOG2E


def _attention(qkv, lq1, lk1, lq2, lk2, subln_g, t):
    S = qkv.shape[0]
    q_cols, k_cols, slopes = _bias_columns(t)
    vec = pl.BlockSpec((1, HEAD_DIM), lambda h, i: (0, 0))
    return pl.pallas_call(
        functools.partial(_attn_kernel, t=t),
        grid=(N_HEADS, S // t),
        in_specs=[pl.BlockSpec(memory_space=pltpu.SMEM),
                  pl.BlockSpec((t, HEAD_WIDTH), lambda h, i: (i, h)),
                  pl.BlockSpec((S, HEAD_WIDTH), lambda h, i: (0, N_HEADS + h)),
                  pl.BlockSpec((S, HEAD_WIDTH), lambda h, i: (0, 2 * N_HEADS + h)),
                  vec,
                  pl.BlockSpec((None, t, HEAD_DIM), lambda h, i: (h, 0, 0)),
                  pl.BlockSpec((t, t), lambda h, i: (0, 0)),
                  vec, vec, vec, vec,
                  pl.BlockSpec((1, HEAD_WIDTH), lambda h, i: (0, 0))],
        out_specs=pl.BlockSpec((t, HEAD_WIDTH), lambda h, i: (i, h)),
        out_shape=jax.ShapeDtypeStruct((S, N_HEADS * HEAD_WIDTH), BF16),
        scratch_shapes=[pltpu.VMEM((2, t, LANES), F32),
                        pltpu.VMEM((2, t, LANES), F32),
                        pltpu.VMEM((2 * t, HEAD_WIDTH), F32),
                        pltpu.VMEM((2, t, t), F32),
                        pltpu.VMEM((2, t, t), F32),
                        pltpu.VMEM((2, t, LANES), F32),
                        pltpu.VMEM((2, t, LANES), F32)],
        compiler_params=_params("arbitrary", "arbitrary"),
        name="attention",
    )(slopes, qkv, qkv, qkv, q_cols, k_cols, _diag_extra(t), lq1, lk1, lq2, lk2, subln_g)


CONV_ROWS = 128
CONV_COLS = 256
SUBLANES = 8


def _conv_kernel(yp_ref, y_ref, w_ref, b_ref, lg_ref, lb_ref, z_ref, ext_ref, xs_ref, cv_ref):
    tm = y_ref.shape[0]
    cb = pl.program_id(1)
    n_cb = pl.num_programs(1)
    first = pl.program_id(0) == 0
    ext_ref[:CONV_HALO, :] = jnp.where(first, 0.0, yp_ref[...].astype(F32))
    ext_ref[CONV_HALO:, :] = y_ref[...].astype(F32)
    shifted_rows = xs_ref.shape[1]
    for r in range(1, SUBLANES):
        xs_ref[r - 1] = ext_ref[r:r + shifted_rows, :]
    lead = CONV_HALO - (CONV_KERNEL - 1)
    for r0 in range(0, tm, CONV_ROWS):
        acc = jnp.broadcast_to(b_ref[...], (CONV_ROWS, CONV_COLS))
        for j in range(CONV_KERNEL):
            r, a = (lead + j) % SUBLANES, (lead + j) // SUBLANES
            src = ext_ref if r == 0 else xs_ref.at[r - 1]
            lo = r0 + SUBLANES * a
            acc = acc + src[lo:lo + CONV_ROWS, :] * w_ref[j:j + 1, :]
        cv_ref[cb, r0:r0 + CONV_ROWS, :] = acc

    @pl.when(cb == n_cb - 1)
    def _():
        n_blocks = cv_ref.shape[0]
        n_ch = n_blocks * CONV_COLS

        def rows_body(rc, c):
            rows = pl.ds(pl.multiple_of(rc * CONV_ROWS, CONV_ROWS), CONV_ROWS)
            blocks = [cv_ref[k, rows, :] for k in range(n_blocks)]
            mu = sum(jnp.sum(b, axis=1, keepdims=True) for b in blocks) / n_ch
            var = sum(jnp.sum((b - mu) * (b - mu), axis=1, keepdims=True) for b in blocks) / n_ch
            inv = lax.rsqrt(var + NORM_EPS)
            for k, b in enumerate(blocks):
                cols = slice(k * CONV_COLS, (k + 1) * CONV_COLS)
                zn = (b - mu) * inv * lg_ref[:, cols] + lb_ref[:, cols]
                z_ref[rows, cols] = (zn * _sigmoid(zn)).astype(z_ref.dtype)
            return c

        lax.fori_loop(0, tm // CONV_ROWS, rows_body, 0)


def _conv_branch(y, dw_w, dw_b, ln_g, ln_b, tm):
    S, C = y.shape
    halo_blocks = tm // CONV_HALO
    n_cb = C // CONV_COLS
    row = pl.BlockSpec((1, C), lambda i, c: (0, 0))
    return pl.pallas_call(
        _conv_kernel,
        grid=(S // tm, n_cb),
        in_specs=[pl.BlockSpec((CONV_HALO, CONV_COLS), lambda i, c: (jnp.maximum(i * halo_blocks - 1, 0), c)),
                  pl.BlockSpec((tm, CONV_COLS), lambda i, c: (i, c)),
                  pl.BlockSpec((CONV_KERNEL, CONV_COLS), lambda i, c: (0, c)),
                  pl.BlockSpec((1, CONV_COLS), lambda i, c: (0, c)),
                  row, row],
        out_specs=pl.BlockSpec((tm, C), lambda i, c: (i, 0)),
        out_shape=jax.ShapeDtypeStruct((S, C), BF16),
        scratch_shapes=[pltpu.VMEM((tm + CONV_HALO, CONV_COLS), F32),
                        pltpu.VMEM((SUBLANES - 1, tm + CONV_HALO - SUBLANES, CONV_COLS), F32),
                        pltpu.VMEM((n_cb, tm, CONV_COLS), F32)],
        compiler_params=_params("arbitrary", "arbitrary"),
        name="conv_branch",
    )(y, y, dw_w, dw_b, ln_g, ln_b)


def _merge_kernel(o_ref, z_ref, wa_ref, wc_ref, bc_ref, ga_ref, gc_ref, out_ref):
    a = jnp.dot(o_ref[...], wa_ref[...], preferred_element_type=F32)
    c = jnp.dot(z_ref[...], wc_ref[...], preferred_element_type=F32) + bc_ref[...]
    out_ref[...] = (ga_ref[...].astype(F32) * a + gc_ref[...].astype(F32) * c).astype(out_ref.dtype)


def _merge(o, z, wa, wc, bc, gates, tm, tn):
    S, D = o.shape
    nb = D // tn
    return pl.pallas_call(
        _merge_kernel,
        grid=(S // tm, nb),
        in_specs=[pl.BlockSpec((tm, D), lambda i, j: (i, 0)),
                  pl.BlockSpec((tm, D), lambda i, j: (i, 0)),
                  pl.BlockSpec((D, tn), lambda i, j: (0, j)),
                  pl.BlockSpec((D, tn), lambda i, j: (0, j)),
                  pl.BlockSpec((1, tn), lambda i, j: (0, j)),
                  pl.BlockSpec((tm, tn), lambda i, j: (i, j)),
                  pl.BlockSpec((tm, tn), lambda i, j: (i, nb + j))],
        out_specs=pl.BlockSpec((tm, tn), lambda i, j: (i, j)),
        out_shape=jax.ShapeDtypeStruct((S, D), BF16),
        compiler_params=_params("arbitrary", "arbitrary"),
        name="merge",
    )(o, z, wa, wc, bc, gates, gates)


def _mixout_kernel(m_ref, w_ref, x_ref, g_ref, h_ref, f_ref):
    h = x_ref[...] + jnp.dot(m_ref[...], w_ref[...], preferred_element_type=F32)
    h_ref[...] = h
    f_ref[...] = _rms_rows(h, g_ref[...]).astype(f_ref.dtype)


def _mixout(mixed, w, x, g, tm):
    S, D = x.shape
    tile = pl.BlockSpec((tm, D), lambda i: (i, 0))
    return pl.pallas_call(
        _mixout_kernel,
        grid=(S // tm,),
        in_specs=[tile,
                  pl.BlockSpec((D, D), lambda i: (0, 0)),
                  tile,
                  pl.BlockSpec((1, D), lambda i: (0, 0))],
        out_specs=[tile, tile],
        out_shape=[jax.ShapeDtypeStruct((S, D), F32), jax.ShapeDtypeStruct((S, D), BF16)],
        compiler_params=_params("arbitrary"),
        name="mixout",
    )(mixed, w, x, g)


def _ffn_in_kernel(f_ref, wg_ref, wu_ref, a_ref):
    f = f_ref[...]
    g = jnp.dot(f, wg_ref[...], preferred_element_type=F32)
    u = jnp.dot(f, wu_ref[...], preferred_element_type=F32)
    a_ref[...] = (g * _sigmoid(g) * u).astype(a_ref.dtype)


def _ffn_in(f, w, tm, tn):
    S, D = f.shape
    d_ff = w.shape[1] // 2
    nb = d_ff // tn
    return pl.pallas_call(
        _ffn_in_kernel,
        grid=(S // tm, nb),
        in_specs=[pl.BlockSpec((tm, D), lambda i, j: (i, 0)),
                  pl.BlockSpec((D, tn), lambda i, j: (0, j)),
                  pl.BlockSpec((D, tn), lambda i, j: (0, nb + j))],
        out_specs=pl.BlockSpec((tm, tn), lambda i, j: (i, j)),
        out_shape=jax.ShapeDtypeStruct((S, d_ff), BF16),
        compiler_params=_params("arbitrary", "arbitrary"),
        name="ffn_in",
    )(f, w, w)


def _ffn_out_kernel(a_ref, w_ref, h_ref, g_ref, out_ref):
    k = pl.program_id(1)

    @pl.when(k == 0)
    def _():
        out_ref[...] = h_ref[...]

    out_ref[...] += jnp.dot(a_ref[...], w_ref[...], preferred_element_type=F32)

    @pl.when(k == pl.num_programs(1) - 1)
    def _():
        out_ref[...] = _rms_rows(out_ref[...], g_ref[...])


def _ffn_out(a, w, h, g, tm, tk):
    S, D = h.shape
    d_ff = a.shape[1]
    return pl.pallas_call(
        _ffn_out_kernel,
        grid=(S // tm, d_ff // tk),
        in_specs=[pl.BlockSpec((tm, tk), lambda i, k: (i, k)),
                  pl.BlockSpec((tk, D), lambda i, k: (k, 0)),
                  pl.BlockSpec((tm, D), lambda i, k: (i, 0)),
                  pl.BlockSpec((1, D), lambda i, k: (0, 0))],
        out_specs=pl.BlockSpec((tm, D), lambda i, k: (i, 0)),
        out_shape=jax.ShapeDtypeStruct((S, D), F32),
        compiler_params=_params("arbitrary", "arbitrary"),
        name="ffn_out",
    )(a, w, h, g)


def _tiles(S, d_ff):
    return dict(
        proj_rows=min(S, 1024), proj_cols=1024, glu_cols=512,
        attn=min(S, 512),
        conv_rows=min(S, 512),
        merge_rows=min(S, 1024), merge_cols=1024,
        mix_rows=min(S, 512),
        ffn_rows=min(S, 1024), ffn_cols=512,
        out_rows=min(S, 512), out_k=d_ff // 4,
    )


def kernel(x, norm_mix_g, w_in, lambda_q1, lambda_k1, lambda_q2, lambda_k2, subln_g, w_attn_out, dw_w, dw_b,
           conv_ln_g, conv_ln_b, w_conv_out, b_conv_out, w_mix_out, norm_ffn_g, w_ffn_in, w_ffn_out, norm_final_g):
    B, S, D = x.shape
    depth = w_in.shape[0]
    assert B == 1 and depth == 1 and D == N_HEADS * HEAD_WIDTH
    d_ff = w_ffn_out.shape[1]
    t = _tiles(S, d_ff)
    row = lambda v: v.reshape(1, -1).astype(F32)

    h0 = x.reshape(S, D)
    w_in_b = w_in[0].astype(BF16)

    q_scale = HEAD_DIM ** -0.5 * LOG2E
    col_scale = jnp.concatenate([jnp.full((1, D), q_scale, F32), jnp.ones((1, 2 * D), F32)], axis=1)
    pc = t["proj_cols"]
    qkv = _norm_proj(_proj_scaled_kernel, h0, row(norm_mix_g[0]), [(w_in_b, 0)], 3 * D // pc, 3 * D,
                     t["proj_rows"], pc, extra=(col_scale,))
    gc = t["glu_cols"]
    y = _norm_proj(_proj_glu_kernel, h0, row(norm_mix_g[0]), [(w_in_b, 3 * D // gc), (w_in_b, 4 * D // gc)],
                   D // gc, D, t["proj_rows"], gc)
    gates = _norm_proj(_proj_sigmoid_kernel, h0, row(norm_mix_g[0]), [(w_in_b, 5 * D // pc)], 2 * D // pc, 2 * D,
                       t["proj_rows"], pc)

    o = _attention(qkv, row(lambda_q1[0]), row(lambda_k1[0]), row(lambda_q2[0]), row(lambda_k2[0]),
                   row(subln_g[0]), t["attn"])
    z = _conv_branch(y, dw_w[0].astype(F32), row(dw_b[0]), row(conv_ln_g[0]), row(conv_ln_b[0]), t["conv_rows"])

    mixed = _merge(o, z, w_attn_out[0].astype(BF16), w_conv_out[0].astype(BF16), row(b_conv_out[0]), gates,
                   t["merge_rows"], t["merge_cols"])
    h1, f = _mixout(mixed, w_mix_out[0].astype(BF16), h0, row(norm_ffn_g[0]), t["mix_rows"])

    a = _ffn_in(f, w_ffn_in[0].astype(BF16), t["ffn_rows"], t["ffn_cols"])
    out = _ffn_out(a, w_ffn_out[0].astype(BF16), h1, row(norm_final_g), t["out_rows"], t["out_k"])
    return out.reshape(B, S, D)
```

```python
import functools
import math

import jax
import jax.numpy as jnp
import numpy as np
from jax import lax
from jax.experimental import pallas as pl
from jax.experimental.pallas import tpu as pltpu

F32 = jnp.float32
BF16 = jnp.bfloat16

N_HEADS = 8
HEAD_DIM = 128
HEAD_WIDTH = 2 * HEAD_DIM
CHUNK = 64
CONV_KERNEL = 31
CONV_HALO = 32
NORM_EPS = 1e-5
LAMBDA_INIT = 0.8 - 0.6 * math.exp(-0.3 * 0)
LOG2E = math.log2(math.e)
LANES = 128
VMEM_LIMIT = 56 * 1024 * 1024
NEG_BIG = -1e30


def _params(*sem):
    return pltpu.CompilerParams(dimension_semantics=sem, vmem_limit_bytes=VMEM_LIMIT)


def _rms_rows(x, g):
    return x * lax.rsqrt(jnp.mean(x * x, axis=-1, keepdims=True) + NORM_EPS) * g


def _sigmoid(x):
    return 1.0 / (1.0 + jnp.exp(-x))


NORM_ROWS = 256


def _norm_into(x_ref, g_ref, u_ref):
    @pl.when(pl.program_id(1) == 0)
    def _():
        def body(r, c):
            rows = pl.ds(pl.multiple_of(r * NORM_ROWS, NORM_ROWS), NORM_ROWS)
            u_ref[rows, :] = _rms_rows(x_ref[rows, :], g_ref[...]).astype(BF16)
            return c

        lax.fori_loop(0, x_ref.shape[0] // NORM_ROWS, body, 0)


def _norm_glu_kernel(x_ref, g_ref, wa_ref, wb_ref, y_ref, u_ref):
    _norm_into(x_ref, g_ref, u_ref)
    u = u_ref[...]
    a = jnp.dot(u, wa_ref[...], preferred_element_type=F32)
    b = jnp.dot(u, wb_ref[...], preferred_element_type=F32)
    y_ref[...] = (a * _sigmoid(b)).astype(y_ref.dtype)


def _norm_glu(x, g, w, first_a, first_b, tm, tn):
    S, D = x.shape
    return pl.pallas_call(
        _norm_glu_kernel,
        grid=(S // tm, D // tn),
        in_specs=[pl.BlockSpec((tm, D), lambda i, j: (i, 0)),
                  pl.BlockSpec((1, D), lambda i, j: (0, 0)),
                  pl.BlockSpec((D, tn), lambda i, j: (0, first_a + j)),
                  pl.BlockSpec((D, tn), lambda i, j: (0, first_b + j))],
        out_specs=[pl.BlockSpec((tm, tn), lambda i, j: (i, j)),
                   pl.BlockSpec((tm, D), lambda i, j: (i, 0))],
        out_shape=[jax.ShapeDtypeStruct((S, D), BF16), jax.ShapeDtypeStruct((S, D), BF16)],
        compiler_params=_params("arbitrary", "arbitrary"),
        name="norm_glu",
    )(x, g, w, w)


def _proj_scaled_kernel(u_ref, w_ref, cs_ref, o_ref):
    r = jnp.dot(u_ref[...], w_ref[...].astype(BF16), preferred_element_type=F32)
    o_ref[...] = (r * cs_ref[...]).astype(o_ref.dtype)


def _proj_scaled(u, w, col_scale, n_out, tm, tn):
    S, D = u.shape
    return pl.pallas_call(
        _proj_scaled_kernel,
        grid=(S // tm, n_out // tn),
        in_specs=[pl.BlockSpec((tm, D), lambda i, j: (i, 0)),
                  pl.BlockSpec((D, tn), lambda i, j: (0, j)),
                  pl.BlockSpec((1, tn), lambda i, j: (0, j))],
        out_specs=pl.BlockSpec((tm, tn), lambda i, j: (i, j)),
        out_shape=jax.ShapeDtypeStruct((S, n_out), BF16),
        compiler_params=_params("arbitrary", "arbitrary"),
        name="proj_qkv",
    )(u, w, col_scale)


def _proj_sigmoid_kernel(u_ref, w_ref, o_ref):
    r = jnp.dot(u_ref[...], w_ref[...].astype(BF16), preferred_element_type=F32)
    o_ref[...] = _sigmoid(r).astype(o_ref.dtype)


def _proj_sigmoid(u, w, first_col_block, n_out, tm, tn):
    S, D = u.shape
    return pl.pallas_call(
        _proj_sigmoid_kernel,
        grid=(S // tm, n_out // tn),
        in_specs=[pl.BlockSpec((tm, D), lambda i, j: (i, 0)),
                  pl.BlockSpec((D, tn), lambda i, j: (0, first_col_block + j))],
        out_specs=pl.BlockSpec((tm, tn), lambda i, j: (i, j)),
        out_shape=jax.ShapeDtypeStruct((S, n_out), BF16),
        compiler_params=_params("arbitrary", "arbitrary"),
        name="proj_gates",
    )(u, w)


STAGE_UNROLL = 4


def _attn_kernel(slope_ref, q_ref, k_ref, v_ref, qaug_ref, kaug_ref, dx_ref, lq1_ref, lk1_ref, lq2_ref, lk2_ref,
                 g_ref, o_ref, m_ref, l_ref, acc_ref, sa_ref, sb_ref, xa_ref, xb_ref, *, t):
    h = pl.program_id(0)
    i = pl.program_id(1)
    slope = slope_ref[h]
    shift = slope * t
    reps = t // LANES

    m_ref[...] = jnp.full(m_ref.shape, NEG_BIG, F32)
    l_ref[...] = jnp.zeros(l_ref.shape, F32)
    acc_ref[...] = jnp.zeros(acc_ref.shape, F32)

    q_extra = jnp.broadcast_to(qaug_ref[...], (t, HEAD_DIM)).astype(BF16)
    q = tuple(jnp.concatenate([q_ref[:, j * HEAD_DIM:(j + 1) * HEAD_DIM], q_extra], axis=1) for j in range(2))

    def block_rows(blk):
        return pl.ds(pl.multiple_of(blk * t, t), t)

    def scores(blk, s_ref, x_ref):
        rows = block_rows(blk)
        for j in range(2):
            k_j = jnp.concatenate([k_ref[rows, j * HEAD_DIM:(j + 1) * HEAD_DIM], kaug_ref[...]], axis=1)
            s = lax.dot_general(q[j], k_j, (((1,), (1,)), ((), ())), preferred_element_type=F32)
            s_ref[j] = s
            lane_max = s[:, :LANES]
            for r in range(1, reps):
                lane_max = jnp.maximum(lane_max, s[:, r * LANES:(r + 1) * LANES])
            x_ref[j] = lane_max

    def softmax_pv(blk, s_ref, x_ref, diagonal):
        ps, alphas = [], []
        for j in range(2):
            if diagonal:
                s = s_ref[j] + slope * dx_ref[...]
                m_cur = jnp.max(s, axis=1, keepdims=True)
            else:
                s = s_ref[j]
                m_cur = jnp.max(x_ref[j], axis=1, keepdims=True)
            m_prev = m_ref[j] - shift
            m_new = jnp.maximum(m_prev, m_cur)
            alpha = jnp.exp2(m_prev - m_new)
            p = jnp.exp2(s - jnp.concatenate([m_new] * reps, axis=1))
            lane_sums = p[:, :LANES]
            for r in range(1, reps):
                lane_sums = lane_sums + p[:, r * LANES:(r + 1) * LANES]
            l_ref[j] = alpha * l_ref[j] + lane_sums
            m_ref[j] = m_new
            ps.append(p.astype(BF16))
            alphas.append(jnp.concatenate([alpha] * (HEAD_WIDTH // LANES), axis=1))
        pv = jnp.dot(jnp.concatenate(ps, axis=0), v_ref[block_rows(blk), :], preferred_element_type=F32)
        acc_ref[...] = acc_ref[...] * jnp.concatenate(alphas, axis=0) + pv

    buf_a = (sa_ref, xa_ref)
    buf_b = (sb_ref, xb_ref)

    def stage(blk, cur, nxt):
        scores(blk + 1, *nxt)
        softmax_pv(blk, *cur, diagonal=False)

    def stage_pair(blk):
        stage(blk, buf_a, buf_b)
        stage(blk + 1, buf_b, buf_a)

    scores(0, *buf_a)

    def unrolled(u, c):
        for d in range(0, STAGE_UNROLL, 2):
            stage_pair(STAGE_UNROLL * u + d)
        return c

    lax.fori_loop(0, i // STAGE_UNROLL, unrolled, 0)
    done = i - i % STAGE_UNROLL

    @pl.when(i % STAGE_UNROLL >= 2)
    def _():
        stage_pair(done)

    odd = i % 2 == 1

    @pl.when(odd)
    def _():
        stage(i - 1, buf_a, buf_b)
        softmax_pv(i, *buf_b, diagonal=True)

    @pl.when(jnp.logical_not(odd))
    def _():
        softmax_pv(i, *buf_a, diagonal=True)

    lam = (jnp.exp(jnp.sum(lq1_ref[...] * lk1_ref[...], axis=1, keepdims=True))
           - jnp.exp(jnp.sum(lq2_ref[...] * lk2_ref[...], axis=1, keepdims=True)) + LAMBDA_INIT)
    o1 = acc_ref[:t, :] / jnp.sum(l_ref[0], axis=1, keepdims=True)
    o2 = acc_ref[t:, :] / jnp.sum(l_ref[1], axis=1, keepdims=True)
    o = o1 - lam * o2
    o_ref[...] = (_rms_rows(o, g_ref[...]) * (1.0 - LAMBDA_INIT)).astype(o_ref.dtype)


BIAS_SPLIT = 256
LOG2E_PIECES = 3


def _diag_extra(t):
    r = lax.broadcasted_iota(jnp.int32, (t, t), 0)
    c = lax.broadcasted_iota(jnp.int32, (t, t), 1)
    allowed = (c // CHUNK) <= (r // CHUNK)
    return jnp.where(allowed, (-2 * jnp.maximum(c - r, 0)).astype(F32), -jnp.inf)


def _bias_columns(t):
    pieces, rest = [], LOG2E
    for _ in range(LOG2E_PIECES):
        p = float(np.asarray(rest, np.float32).astype(BF16).astype(np.float32))
        pieces.append(p)
        rest -= p
    q_cols = jnp.zeros((1, HEAD_DIM), F32).at[0, :2 * LOG2E_PIECES].set(jnp.asarray(pieces * 2, F32))
    slopes = jnp.exp2(-(jnp.arange(N_HEADS, dtype=F32) + 1.0) * (8.0 / N_HEADS))
    c = jnp.arange(t, dtype=jnp.int32)
    lo = (c % BIAS_SPLIT).astype(F32)
    hi = (c - c % BIAS_SPLIT).astype(F32)
    k_cols = jnp.concatenate([jnp.tile(lo[:, None], (1, LOG2E_PIECES)), jnp.tile(hi[:, None], (1, LOG2E_PIECES)),
                              jnp.zeros((t, HEAD_DIM - 2 * LOG2E_PIECES), F32)], axis=1)
    k_cols = (slopes[:, None, None] * k_cols[None]).astype(BF16)
    return q_cols, k_cols, slopes * LOG2E


def _attention(qkv, lq1, lk1, lq2, lk2, subln_g, t):
    S = qkv.shape[0]
    q_cols, k_cols, slopes = _bias_columns(t)
    vec = pl.BlockSpec((1, HEAD_DIM), lambda h, i: (0, 0))
    return pl.pallas_call(
        functools.partial(_attn_kernel, t=t),
        grid=(N_HEADS, S // t),
        in_specs=[pl.BlockSpec(memory_space=pltpu.SMEM),
                  pl.BlockSpec((t, HEAD_WIDTH), lambda h, i: (i, h)),
                  pl.BlockSpec((S, HEAD_WIDTH), lambda h, i: (0, N_HEADS + h)),
                  pl.BlockSpec((S, HEAD_WIDTH), lambda h, i: (0, 2 * N_HEADS + h)),
                  vec,
                  pl.BlockSpec((None, t, HEAD_DIM), lambda h, i: (h, 0, 0)),
                  pl.BlockSpec((t, t), lambda h, i: (0, 0)),
                  vec, vec, vec, vec,
                  pl.BlockSpec((1, HEAD_WIDTH), lambda h, i: (0, 0))],
        out_specs=pl.BlockSpec((t, HEAD_WIDTH), lambda h, i: (i, h)),
        out_shape=jax.ShapeDtypeStruct((S, N_HEADS * HEAD_WIDTH), BF16),
        scratch_shapes=[pltpu.VMEM((2, t, LANES), F32),
                        pltpu.VMEM((2, t, LANES), F32),
                        pltpu.VMEM((2 * t, HEAD_WIDTH), F32),
                        pltpu.VMEM((2, t, t), F32),
                        pltpu.VMEM((2, t, t), F32),
                        pltpu.VMEM((2, t, LANES), F32),
                        pltpu.VMEM((2, t, LANES), F32)],
        compiler_params=_params("arbitrary", "arbitrary"),
        name="attention",
    )(slopes, qkv, qkv, qkv, q_cols, k_cols, _diag_extra(t), lq1, lk1, lq2, lk2, subln_g)


CONV_ROWS = 128
CONV_COLS = 256
SUBLANES = 8


def _conv_kernel(yp_ref, y_ref, w_ref, b_ref, lg_ref, lb_ref, z_ref, ext_ref, xs_ref, cv_ref):
    tm = y_ref.shape[0]
    cb = pl.program_id(1)
    n_cb = pl.num_programs(1)
    first = pl.program_id(0) == 0
    ext_ref[:CONV_HALO, :] = jnp.where(first, 0.0, yp_ref[...].astype(F32))
    ext_ref[CONV_HALO:, :] = y_ref[...].astype(F32)
    shifted_rows = xs_ref.shape[1]
    for r in range(1, SUBLANES):
        xs_ref[r - 1] = ext_ref[r:r + shifted_rows, :]
    lead = CONV_HALO - (CONV_KERNEL - 1)
    for r0 in range(0, tm, CONV_ROWS):
        acc = jnp.broadcast_to(b_ref[...], (CONV_ROWS, CONV_COLS))
        for j in range(CONV_KERNEL):
            r, a = (lead + j) % SUBLANES, (lead + j) // SUBLANES
            src = ext_ref if r == 0 else xs_ref.at[r - 1]
            lo = r0 + SUBLANES * a
            acc = acc + src[lo:lo + CONV_ROWS, :] * w_ref[j:j + 1, :]
        cv_ref[cb, r0:r0 + CONV_ROWS, :] = acc

    @pl.when(cb == n_cb - 1)
    def _():
        n_blocks = cv_ref.shape[0]
        n_ch = n_blocks * CONV_COLS

        def rows_body(rc, c):
            rows = pl.ds(pl.multiple_of(rc * CONV_ROWS, CONV_ROWS), CONV_ROWS)
            blocks = [cv_ref[k, rows, :] for k in range(n_blocks)]
            mu = sum(jnp.sum(b, axis=1, keepdims=True) for b in blocks) / n_ch
            var = sum(jnp.sum((b - mu) * (b - mu), axis=1, keepdims=True) for b in blocks) / n_ch
            inv = lax.rsqrt(var + NORM_EPS)
            for k, b in enumerate(blocks):
                cols = slice(k * CONV_COLS, (k + 1) * CONV_COLS)
                zn = (b - mu) * inv * lg_ref[:, cols] + lb_ref[:, cols]
                z_ref[rows, cols] = (zn * _sigmoid(zn)).astype(z_ref.dtype)
            return c

        lax.fori_loop(0, tm // CONV_ROWS, rows_body, 0)


def _conv_branch(y, dw_w, dw_b, ln_g, ln_b, tm):
    S, C = y.shape
    halo_blocks = tm // CONV_HALO
    n_cb = C // CONV_COLS
    row = pl.BlockSpec((1, C), lambda i, c: (0, 0))
    return pl.pallas_call(
        _conv_kernel,
        grid=(S // tm, n_cb),
        in_specs=[pl.BlockSpec((CONV_HALO, CONV_COLS), lambda i, c: (jnp.maximum(i * halo_blocks - 1, 0), c)),
                  pl.BlockSpec((tm, CONV_COLS), lambda i, c: (i, c)),
                  pl.BlockSpec((CONV_KERNEL, CONV_COLS), lambda i, c: (0, c)),
                  pl.BlockSpec((1, CONV_COLS), lambda i, c: (0, c)),
                  row, row],
        out_specs=pl.BlockSpec((tm, C), lambda i, c: (i, 0)),
        out_shape=jax.ShapeDtypeStruct((S, C), BF16),
        scratch_shapes=[pltpu.VMEM((tm + CONV_HALO, CONV_COLS), F32),
                        pltpu.VMEM((SUBLANES - 1, tm + CONV_HALO - SUBLANES, CONV_COLS), F32),
                        pltpu.VMEM((n_cb, tm, CONV_COLS), F32)],
        compiler_params=_params("arbitrary", "arbitrary"),
        name="conv_branch",
    )(y, y, dw_w, dw_b, ln_g, ln_b)


def _merge_kernel(o_ref, z_ref, wa_ref, wc_ref, bc_ref, ga_ref, gc_ref, out_ref):
    a = jnp.dot(o_ref[...], wa_ref[...], preferred_element_type=F32)
    c = jnp.dot(z_ref[...], wc_ref[...], preferred_element_type=F32) + bc_ref[...]
    out_ref[...] = (ga_ref[...].astype(F32) * a + gc_ref[...].astype(F32) * c).astype(out_ref.dtype)


def _merge(o, z, wa, wc, bc, gates, tm, tn):
    S, D = o.shape
    nb = D // tn
    return pl.pallas_call(
        _merge_kernel,
        grid=(S // tm, nb),
        in_specs=[pl.BlockSpec((tm, D), lambda i, j: (i, 0)),
                  pl.BlockSpec((tm, D), lambda i, j: (i, 0)),
                  pl.BlockSpec((D, tn), lambda i, j: (0, j)),
                  pl.BlockSpec((D, tn), lambda i, j: (0, j)),
                  pl.BlockSpec((1, tn), lambda i, j: (0, j)),
                  pl.BlockSpec((tm, tn), lambda i, j: (i, j)),
                  pl.BlockSpec((tm, tn), lambda i, j: (i, nb + j))],
        out_specs=pl.BlockSpec((tm, tn), lambda i, j: (i, j)),
        out_shape=jax.ShapeDtypeStruct((S, D), BF16),
        compiler_params=_params("arbitrary", "arbitrary"),
        name="merge",
    )(o, z, wa, wc, bc, gates, gates)


def _mixout_kernel(m_ref, w_ref, x_ref, g_ref, h_ref, f_ref):
    h = x_ref[...] + jnp.dot(m_ref[...], w_ref[...], preferred_element_type=F32)
    h_ref[...] = h
    f_ref[...] = _rms_rows(h, g_ref[...]).astype(f_ref.dtype)


def _mixout(mixed, w, x, g, tm):
    S, D = x.shape
    tile = pl.BlockSpec((tm, D), lambda i: (i, 0))
    return pl.pallas_call(
        _mixout_kernel,
        grid=(S // tm,),
        in_specs=[tile,
                  pl.BlockSpec((D, D), lambda i: (0, 0)),
                  tile,
                  pl.BlockSpec((1, D), lambda i: (0, 0))],
        out_specs=[tile, tile],
        out_shape=[jax.ShapeDtypeStruct((S, D), F32), jax.ShapeDtypeStruct((S, D), BF16)],
        compiler_params=_params("arbitrary"),
        name="mixout",
    )(mixed, w, x, g)


def _ffn_in_kernel(f_ref, wg_ref, wu_ref, a_ref):
    f = f_ref[...]
    g = jnp.dot(f, wg_ref[...].astype(BF16), preferred_element_type=F32)
    u = jnp.dot(f, wu_ref[...].astype(BF16), preferred_element_type=F32)
    a_ref[...] = (g * _sigmoid(g) * u).astype(a_ref.dtype)


def _ffn_in(f, w, tm, tn):
    S, D = f.shape
    d_ff = w.shape[1] // 2
    nb = d_ff // tn
    return pl.pallas_call(
        _ffn_in_kernel,
        grid=(S // tm, nb),
        in_specs=[pl.BlockSpec((tm, D), lambda i, j: (i, 0)),
                  pl.BlockSpec((D, tn), lambda i, j: (0, j)),
                  pl.BlockSpec((D, tn), lambda i, j: (0, nb + j))],
        out_specs=pl.BlockSpec((tm, tn), lambda i, j: (i, j)),
        out_shape=jax.ShapeDtypeStruct((S, d_ff), BF16),
        compiler_params=_params("arbitrary", "arbitrary"),
        name="ffn_in",
    )(f, w, w)


def _ffn_out_kernel(a_ref, w_ref, h_ref, g_ref, out_ref):
    k = pl.program_id(1)

    @pl.when(k == 0)
    def _():
        out_ref[...] = h_ref[...]

    out_ref[...] += jnp.dot(a_ref[...], w_ref[...], preferred_element_type=F32)

    @pl.when(k == pl.num_programs(1) - 1)
    def _():
        out_ref[...] = _rms_rows(out_ref[...], g_ref[...])


def _ffn_out(a, w, h, g, tm, tk):
    S, D = h.shape
    d_ff = a.shape[1]
    return pl.pallas_call(
        _ffn_out_kernel,
        grid=(S // tm, d_ff // tk),
        in_specs=[pl.BlockSpec((tm, tk), lambda i, k: (i, k)),
                  pl.BlockSpec((tk, D), lambda i, k: (k, 0)),
                  pl.BlockSpec((tm, D), lambda i, k: (i, 0)),
                  pl.BlockSpec((1, D), lambda i, k: (0, 0))],
        out_specs=pl.BlockSpec((tm, D), lambda i, k: (i, 0)),
        out_shape=jax.ShapeDtypeStruct((S, D), F32),
        compiler_params=_params("arbitrary", "arbitrary"),
        name="ffn_out",
    )(a, w, h, g)


def _tiles(S, d_ff):
    return dict(
        proj_rows=min(S, 1024), proj_cols=1024, glu_cols=512,
        attn=min(S, 512),
        conv_rows=min(S, 512),
        merge_rows=min(S, 1024), merge_cols=1024,
        mix_rows=min(S, 512),
        ffn_rows=min(S, 1024), ffn_cols=512,
        out_rows=min(S, 1024), out_k=d_ff // 4,
    )


def kernel(x, norm_mix_g, w_in, lambda_q1, lambda_k1, lambda_q2, lambda_k2, subln_g, w_attn_out, dw_w, dw_b,
           conv_ln_g, conv_ln_b, w_conv_out, b_conv_out, w_mix_out, norm_ffn_g, w_ffn_in, w_ffn_out, norm_final_g):
    B, S, D = x.shape
    depth = w_in.shape[0]
    assert B == 1 and depth == 1 and D == N_HEADS * HEAD_WIDTH
    d_ff = w_ffn_out.shape[1]
    t = _tiles(S, d_ff)
    row = lambda v: v.reshape(1, -1).astype(F32)

    h0 = x.reshape(S, D)
    w_in_f = w_in[0]

    q_scale = HEAD_DIM ** -0.5 * LOG2E
    col_scale = jnp.concatenate([jnp.full((1, D), q_scale, F32), jnp.ones((1, 2 * D), F32)], axis=1)
    pc = t["proj_cols"]
    gc = t["glu_cols"]
    w_glu = w_in_f[:, 3 * D:5 * D].astype(BF16)
    y, u = _norm_glu(h0, row(norm_mix_g[0]), w_glu, 0, D // gc, t["proj_rows"], gc)
    qkv = _proj_scaled(u, w_in_f, col_scale, 3 * D, t["proj_rows"], pc)
    gates = _proj_sigmoid(u, w_in_f, 5 * D // pc, 2 * D, t["proj_rows"], pc)

    o = _attention(qkv, row(lambda_q1[0]), row(lambda_k1[0]), row(lambda_q2[0]), row(lambda_k2[0]),
                   row(subln_g[0]), t["attn"])
    z = _conv_branch(y, dw_w[0].astype(F32), row(dw_b[0]), row(conv_ln_g[0]), row(conv_ln_b[0]), t["conv_rows"])

    mixed = _merge(o, z, w_attn_out[0].astype(BF16), w_conv_out[0].astype(BF16), row(b_conv_out[0]), gates,
                   t["merge_rows"], t["merge_cols"])
    h1, f = _mixout(mixed, w_mix_out[0].astype(BF16), h0, row(norm_ffn_g[0]), t["mix_rows"])

    a = _ffn_in(f, w_ffn_in[0], t["ffn_rows"], t["ffn_cols"])
    out = _ffn_out(a, w_ffn_out[0].astype(BF16), h1, row(norm_final_g), t["out_rows"], t["out_k"])
    return out.reshape(B, S, D)
```

```python
import functools
import math

import jax
import jax.numpy as jnp
import numpy as np
from jax import lax
from jax.experimental import pallas as pl
from jax.experimental.pallas import tpu as pltpu

F32 = jnp.float32
BF16 = jnp.bfloat16

N_HEADS = 8
HEAD_DIM = 128
HEAD_WIDTH = 2 * HEAD_DIM
CHUNK = 64
CONV_KERNEL = 31
CONV_HALO = 32
NORM_EPS = 1e-5
LAMBDA_INIT = 0.8 - 0.6 * math.exp(-0.3 * 0)
LOG2E = math.log2(math.e)
LANES = 128
VMEM_LIMIT = 56 * 1024 * 1024
NEG_BIG = -1e30


def _params(*sem):
    return pltpu.CompilerParams(dimension_semantics=sem, vmem_limit_bytes=VMEM_LIMIT)


def _rms_rows(x, g):
    return x * lax.rsqrt(jnp.mean(x * x, axis=-1, keepdims=True) + NORM_EPS) * g


def _sigmoid(x):
    return 1.0 / (1.0 + jnp.exp(-x))


NORM_ROWS = 256


def _norm_into(x_ref, g_ref, u_ref):
    @pl.when(pl.program_id(1) == 0)
    def _():
        def body(r, c):
            rows = pl.ds(pl.multiple_of(r * NORM_ROWS, NORM_ROWS), NORM_ROWS)
            u_ref[rows, :] = _rms_rows(x_ref[rows, :], g_ref[...]).astype(BF16)
            return c

        lax.fori_loop(0, x_ref.shape[0] // NORM_ROWS, body, 0)


def _norm_glu_kernel(x_ref, g_ref, wa_ref, wb_ref, y_ref, u_ref):
    _norm_into(x_ref, g_ref, u_ref)
    u = u_ref[...]
    a = jnp.dot(u, wa_ref[...], preferred_element_type=F32)
    b = jnp.dot(u, wb_ref[...], preferred_element_type=F32)
    y_ref[...] = (a * _sigmoid(b)).astype(y_ref.dtype)


def _norm_glu(x, g, w, first_a, first_b, tm, tn):
    S, D = x.shape
    return pl.pallas_call(
        _norm_glu_kernel,
        grid=(S // tm, D // tn),
        in_specs=[pl.BlockSpec((tm, D), lambda i, j: (i, 0)),
                  pl.BlockSpec((1, D), lambda i, j: (0, 0)),
                  pl.BlockSpec((D, tn), lambda i, j: (0, first_a + j)),
                  pl.BlockSpec((D, tn), lambda i, j: (0, first_b + j))],
        out_specs=[pl.BlockSpec((tm, tn), lambda i, j: (i, j)),
                   pl.BlockSpec((tm, D), lambda i, j: (i, 0))],
        out_shape=[jax.ShapeDtypeStruct((S, D), BF16), jax.ShapeDtypeStruct((S, D), BF16)],
        compiler_params=_params("arbitrary", "arbitrary"),
        name="norm_glu",
    )(x, g, w, w)


def _proj_scaled_kernel(u_ref, w_ref, cs_ref, o_ref):
    r = jnp.dot(u_ref[...], w_ref[...].astype(BF16), preferred_element_type=F32)
    o_ref[...] = (r * cs_ref[...]).astype(o_ref.dtype)


def _proj_scaled(u, w, col_scale, n_out, tm, tn):
    S, D = u.shape
    return pl.pallas_call(
        _proj_scaled_kernel,
        grid=(S // tm, n_out // tn),
        in_specs=[pl.BlockSpec((tm, D), lambda i, j: (i, 0)),
                  pl.BlockSpec((D, tn), lambda i, j: (0, j)),
                  pl.BlockSpec((1, tn), lambda i, j: (0, j))],
        out_specs=pl.BlockSpec((tm, tn), lambda i, j: (i, j)),
        out_shape=jax.ShapeDtypeStruct((S, n_out), BF16),
        compiler_params=_params("arbitrary", "arbitrary"),
        name="proj_qkv",
    )(u, w, col_scale)


def _proj_sigmoid_kernel(u_ref, w_ref, o_ref):
    r = jnp.dot(u_ref[...], w_ref[...].astype(BF16), preferred_element_type=F32)
    o_ref[...] = _sigmoid(r).astype(o_ref.dtype)


def _proj_sigmoid(u, w, first_col_block, n_out, tm, tn):
    S, D = u.shape
    return pl.pallas_call(
        _proj_sigmoid_kernel,
        grid=(S // tm, n_out // tn),
        in_specs=[pl.BlockSpec((tm, D), lambda i, j: (i, 0)),
                  pl.BlockSpec((D, tn), lambda i, j: (0, first_col_block + j))],
        out_specs=pl.BlockSpec((tm, tn), lambda i, j: (i, j)),
        out_shape=jax.ShapeDtypeStruct((S, n_out), BF16),
        compiler_params=_params("arbitrary", "arbitrary"),
        name="proj_gates",
    )(u, w)


STAGE_UNROLL = 8


def _attn_kernel(slope_ref, q_ref, k_ref, v_ref, qaug_ref, kaug_ref, dx_ref, lq1_ref, lk1_ref, lq2_ref, lk2_ref,
                 g_ref, o_ref, m_ref, l_ref, acc_ref, sa_ref, sb_ref, xa_ref, xb_ref, *, t):
    h = pl.program_id(0)
    i = pl.program_id(1)
    slope = slope_ref[h]
    shift = slope * t
    reps = t // LANES

    m_ref[...] = jnp.full(m_ref.shape, NEG_BIG, F32)
    l_ref[...] = jnp.zeros(l_ref.shape, F32)
    acc_ref[...] = jnp.zeros(acc_ref.shape, F32)

    q_extra = jnp.broadcast_to(qaug_ref[...], (t, HEAD_DIM)).astype(BF16)
    q = tuple(jnp.concatenate([q_ref[:, j * HEAD_DIM:(j + 1) * HEAD_DIM], q_extra], axis=1) for j in range(2))

    def block_rows(blk):
        return pl.ds(pl.multiple_of(blk * t, t), t)

    def scores(blk, s_ref, x_ref):
        rows = block_rows(blk)
        for j in range(2):
            k_j = jnp.concatenate([k_ref[rows, j * HEAD_DIM:(j + 1) * HEAD_DIM], kaug_ref[...]], axis=1)
            s = lax.dot_general(q[j], k_j, (((1,), (1,)), ((), ())), preferred_element_type=F32)
            s_ref[j] = s
            lane_max = s[:, :LANES]
            for r in range(1, reps):
                lane_max = jnp.maximum(lane_max, s[:, r * LANES:(r + 1) * LANES])
            x_ref[j] = lane_max

    def softmax_pv(blk, s_ref, x_ref, diagonal):
        ps, alphas = [], []
        for j in range(2):
            if diagonal:
                s = s_ref[j] + slope * dx_ref[...]
                m_cur = jnp.max(s, axis=1, keepdims=True)
            else:
                s = s_ref[j]
                m_cur = jnp.max(x_ref[j], axis=1, keepdims=True)
            m_prev = m_ref[j] - shift
            m_new = jnp.maximum(m_prev, m_cur)
            alpha = jnp.exp2(m_prev - m_new)
            p = jnp.exp2(s - jnp.concatenate([m_new] * reps, axis=1))
            lane_sums = p[:, :LANES]
            for r in range(1, reps):
                lane_sums = lane_sums + p[:, r * LANES:(r + 1) * LANES]
            l_ref[j] = alpha * l_ref[j] + lane_sums
            m_ref[j] = m_new
            ps.append(p.astype(BF16))
            alphas.append(jnp.concatenate([alpha] * (HEAD_WIDTH // LANES), axis=1))
        pv = jnp.dot(jnp.concatenate(ps, axis=0), v_ref[block_rows(blk), :], preferred_element_type=F32)
        acc_ref[...] = acc_ref[...] * jnp.concatenate(alphas, axis=0) + pv

    buf_a = (sa_ref, xa_ref)
    buf_b = (sb_ref, xb_ref)

    def stage(blk, cur, nxt):
        scores(blk + 1, *nxt)
        softmax_pv(blk, *cur, diagonal=False)

    def stage_pair(blk):
        stage(blk, buf_a, buf_b)
        stage(blk + 1, buf_b, buf_a)

    scores(0, *buf_a)

    def stages(first, count):
        for d in range(0, count, 2):
            stage_pair(first + d)

    def unrolled(u, c):
        stages(STAGE_UNROLL * u, STAGE_UNROLL)
        return c

    lax.fori_loop(0, i // STAGE_UNROLL, unrolled, 0)
    piece = STAGE_UNROLL // 2
    while piece >= 2:
        @pl.when((i & piece) != 0)
        def _(piece=piece):
            stages(i & ~(2 * piece - 1), piece)

        piece //= 2

    odd = i % 2 == 1

    @pl.when(odd)
    def _():
        stage(i - 1, buf_a, buf_b)
        softmax_pv(i, *buf_b, diagonal=True)

    @pl.when(jnp.logical_not(odd))
    def _():
        softmax_pv(i, *buf_a, diagonal=True)

    lam = (jnp.exp(jnp.sum(lq1_ref[...] * lk1_ref[...], axis=1, keepdims=True))
           - jnp.exp(jnp.sum(lq2_ref[...] * lk2_ref[...], axis=1, keepdims=True)) + LAMBDA_INIT)
    o1 = acc_ref[:t, :] / jnp.sum(l_ref[0], axis=1, keepdims=True)
    o2 = acc_ref[t:, :] / jnp.sum(l_ref[1], axis=1, keepdims=True)
    o = o1 - lam * o2
    o_ref[...] = (_rms_rows(o, g_ref[...]) * (1.0 - LAMBDA_INIT)).astype(o_ref.dtype)


BIAS_SPLIT = 256
LOG2E_PIECES = 3


def _diag_extra(t):
    r = lax.broadcasted_iota(jnp.int32, (t, t), 0)
    c = lax.broadcasted_iota(jnp.int32, (t, t), 1)
    allowed = (c // CHUNK) <= (r // CHUNK)
    return jnp.where(allowed, (-2 * jnp.maximum(c - r, 0)).astype(F32), -jnp.inf)


def _bias_columns(t):
    pieces, rest = [], LOG2E
    for _ in range(LOG2E_PIECES):
        p = float(np.asarray(rest, np.float32).astype(BF16).astype(np.float32))
        pieces.append(p)
        rest -= p
    q_cols = jnp.zeros((1, HEAD_DIM), F32).at[0, :2 * LOG2E_PIECES].set(jnp.asarray(pieces * 2, F32))
    slopes = jnp.exp2(-(jnp.arange(N_HEADS, dtype=F32) + 1.0) * (8.0 / N_HEADS))
    c = jnp.arange(t, dtype=jnp.int32)
    lo = (c % BIAS_SPLIT).astype(F32)
    hi = (c - c % BIAS_SPLIT).astype(F32)
    k_cols = jnp.concatenate([jnp.tile(lo[:, None], (1, LOG2E_PIECES)), jnp.tile(hi[:, None], (1, LOG2E_PIECES)),
                              jnp.zeros((t, HEAD_DIM - 2 * LOG2E_PIECES), F32)], axis=1)
    k_cols = (slopes[:, None, None] * k_cols[None]).astype(BF16)
    return q_cols, k_cols, slopes * LOG2E


def _attention(qkv, lq1, lk1, lq2, lk2, subln_g, t):
    S = qkv.shape[0]
    q_cols, k_cols, slopes = _bias_columns(t)
    vec = pl.BlockSpec((1, HEAD_DIM), lambda h, i: (0, 0))
    return pl.pallas_call(
        functools.partial(_attn_kernel, t=t),
        grid=(N_HEADS, S // t),
        in_specs=[pl.BlockSpec(memory_space=pltpu.SMEM),
                  pl.BlockSpec((t, HEAD_WIDTH), lambda h, i: (i, h)),
                  pl.BlockSpec((S, HEAD_WIDTH), lambda h, i: (0, N_HEADS + h)),
                  pl.BlockSpec((S, HEAD_WIDTH), lambda h, i: (0, 2 * N_HEADS + h)),
                  vec,
                  pl.BlockSpec((None, t, HEAD_DIM), lambda h, i: (h, 0, 0)),
                  pl.BlockSpec((t, t), lambda h, i: (0, 0)),
                  vec, vec, vec, vec,
                  pl.BlockSpec((1, HEAD_WIDTH), lambda h, i: (0, 0))],
        out_specs=pl.BlockSpec((t, HEAD_WIDTH), lambda h, i: (i, h)),
        out_shape=jax.ShapeDtypeStruct((S, N_HEADS * HEAD_WIDTH), BF16),
        scratch_shapes=[pltpu.VMEM((2, t, LANES), F32),
                        pltpu.VMEM((2, t, LANES), F32),
                        pltpu.VMEM((2 * t, HEAD_WIDTH), F32),
                        pltpu.VMEM((2, t, t), F32),
                        pltpu.VMEM((2, t, t), F32),
                        pltpu.VMEM((2, t, LANES), F32),
                        pltpu.VMEM((2, t, LANES), F32)],
        compiler_params=_params("arbitrary", "arbitrary"),
        name="attention",
    )(slopes, qkv, qkv, qkv, q_cols, k_cols, _diag_extra(t), lq1, lk1, lq2, lk2, subln_g)


CONV_ROWS = 128
CONV_COLS = 256
SUBLANES = 8


def _conv_kernel(yp_ref, y_ref, w_ref, b_ref, lg_ref, lb_ref, z_ref, ext_ref, xs_ref, cv_ref):
    tm = y_ref.shape[0]
    cb = pl.program_id(1)
    n_cb = pl.num_programs(1)
    first = pl.program_id(0) == 0
    ext_ref[:CONV_HALO, :] = jnp.where(first, 0.0, yp_ref[...].astype(F32))
    ext_ref[CONV_HALO:, :] = y_ref[...].astype(F32)
    shifted_rows = xs_ref.shape[1]
    for r in range(1, SUBLANES):
        xs_ref[r - 1] = ext_ref[r:r + shifted_rows, :]
    lead = CONV_HALO - (CONV_KERNEL - 1)
    for r0 in range(0, tm, CONV_ROWS):
        acc = jnp.broadcast_to(b_ref[...], (CONV_ROWS, CONV_COLS))
        for j in range(CONV_KERNEL):
            r, a = (lead + j) % SUBLANES, (lead + j) // SUBLANES
            src = ext_ref if r == 0 else xs_ref.at[r - 1]
            lo = r0 + SUBLANES * a
            acc = acc + src[lo:lo + CONV_ROWS, :] * w_ref[j:j + 1, :]
        cv_ref[cb, r0:r0 + CONV_ROWS, :] = acc

    @pl.when(cb == n_cb - 1)
    def _():
        n_blocks = cv_ref.shape[0]
        n_ch = n_blocks * CONV_COLS

        def rows_body(rc, c):
            rows = pl.ds(pl.multiple_of(rc * CONV_ROWS, CONV_ROWS), CONV_ROWS)
            blocks = [cv_ref[k, rows, :] for k in range(n_blocks)]
            mu = sum(jnp.sum(b, axis=1, keepdims=True) for b in blocks) / n_ch
            var = sum(jnp.sum((b - mu) * (b - mu), axis=1, keepdims=True) for b in blocks) / n_ch
            inv = lax.rsqrt(var + NORM_EPS)
            for k, b in enumerate(blocks):
                cols = slice(k * CONV_COLS, (k + 1) * CONV_COLS)
                zn = (b - mu) * inv * lg_ref[:, cols] + lb_ref[:, cols]
                z_ref[rows, cols] = (zn * _sigmoid(zn)).astype(z_ref.dtype)
            return c

        lax.fori_loop(0, tm // CONV_ROWS, rows_body, 0)


def _conv_branch(y, dw_w, dw_b, ln_g, ln_b, tm):
    S, C = y.shape
    halo_blocks = tm // CONV_HALO
    n_cb = C // CONV_COLS
    row = pl.BlockSpec((1, C), lambda i, c: (0, 0))
    return pl.pallas_call(
        _conv_kernel,
        grid=(S // tm, n_cb),
        in_specs=[pl.BlockSpec((CONV_HALO, CONV_COLS), lambda i, c: (jnp.maximum(i * halo_blocks - 1, 0), c)),
                  pl.BlockSpec((tm, CONV_COLS), lambda i, c: (i, c)),
                  pl.BlockSpec((CONV_KERNEL, CONV_COLS), lambda i, c: (0, c)),
                  pl.BlockSpec((1, CONV_COLS), lambda i, c: (0, c)),
                  row, row],
        out_specs=pl.BlockSpec((tm, C), lambda i, c: (i, 0)),
        out_shape=jax.ShapeDtypeStruct((S, C), BF16),
        scratch_shapes=[pltpu.VMEM((tm + CONV_HALO, CONV_COLS), F32),
                        pltpu.VMEM((SUBLANES - 1, tm + CONV_HALO - SUBLANES, CONV_COLS), F32),
                        pltpu.VMEM((n_cb, tm, CONV_COLS), F32)],
        compiler_params=_params("arbitrary", "arbitrary"),
        name="conv_branch",
    )(y, y, dw_w, dw_b, ln_g, ln_b)


def _merge_kernel(o_ref, z_ref, wa_ref, wc_ref, bc_ref, ga_ref, gc_ref, out_ref):
    a = jnp.dot(o_ref[...], wa_ref[...], preferred_element_type=F32)
    c = jnp.dot(z_ref[...], wc_ref[...], preferred_element_type=F32) + bc_ref[...]
    out_ref[...] = (ga_ref[...].astype(F32) * a + gc_ref[...].astype(F32) * c).astype(out_ref.dtype)


def _merge(o, z, wa, wc, bc, gates, tm, tn):
    S, D = o.shape
    nb = D // tn
    return pl.pallas_call(
        _merge_kernel,
        grid=(S // tm, nb),
        in_specs=[pl.BlockSpec((tm, D), lambda i, j: (i, 0)),
                  pl.BlockSpec((tm, D), lambda i, j: (i, 0)),
                  pl.BlockSpec((D, tn), lambda i, j: (0, j)),
                  pl.BlockSpec((D, tn), lambda i, j: (0, j)),
                  pl.BlockSpec((1, tn), lambda i, j: (0, j)),
                  pl.BlockSpec((tm, tn), lambda i, j: (i, j)),
                  pl.BlockSpec((tm, tn), lambda i, j: (i, nb + j))],
        out_specs=pl.BlockSpec((tm, tn), lambda i, j: (i, j)),
        out_shape=jax.ShapeDtypeStruct((S, D), BF16),
        compiler_params=_params("arbitrary", "arbitrary"),
        name="merge",
    )(o, z, wa, wc, bc, gates, gates)


def _mixout_kernel(m_ref, w_ref, x_ref, g_ref, h_ref, f_ref):
    h = x_ref[...] + jnp.dot(m_ref[...], w_ref[...], preferred_element_type=F32)
    h_ref[...] = h
    f_ref[...] = _rms_rows(h, g_ref[...]).astype(f_ref.dtype)


def _mixout(mixed, w, x, g, tm):
    S, D = x.shape
    tile = pl.BlockSpec((tm, D), lambda i: (i, 0))
    return pl.pallas_call(
        _mixout_kernel,
        grid=(S // tm,),
        in_specs=[tile,
                  pl.BlockSpec((D, D), lambda i: (0, 0)),
                  tile,
                  pl.BlockSpec((1, D), lambda i: (0, 0))],
        out_specs=[tile, tile],
        out_shape=[jax.ShapeDtypeStruct((S, D), F32), jax.ShapeDtypeStruct((S, D), BF16)],
        compiler_params=_params("arbitrary"),
        name="mixout",
    )(mixed, w, x, g)


def _ffn_in_kernel(f_ref, wg_ref, wu_ref, a_ref):
    f = f_ref[...]
    g = jnp.dot(f, wg_ref[...].astype(BF16), preferred_element_type=F32)
    u = jnp.dot(f, wu_ref[...].astype(BF16), preferred_element_type=F32)
    a_ref[...] = (g * _sigmoid(g) * u).astype(a_ref.dtype)


def _ffn_in(f, w, tm, tn):
    S, D = f.shape
    d_ff = w.shape[1] // 2
    nb = d_ff // tn
    return pl.pallas_call(
        _ffn_in_kernel,
        grid=(S // tm, nb),
        in_specs=[pl.BlockSpec((tm, D), lambda i, j: (i, 0)),
                  pl.BlockSpec((D, tn), lambda i, j: (0, j)),
                  pl.BlockSpec((D, tn), lambda i, j: (0, nb + j))],
        out_specs=pl.BlockSpec((tm, tn), lambda i, j: (i, j)),
        out_shape=jax.ShapeDtypeStruct((S, d_ff), BF16),
        compiler_params=_params("arbitrary", "arbitrary"),
        name="ffn_in",
    )(f, w, w)


def _ffn_out_kernel(a_ref, w_ref, h_ref, g_ref, out_ref):
    k = pl.program_id(1)

    @pl.when(k == 0)
    def _():
        out_ref[...] = h_ref[...]

    out_ref[...] += jnp.dot(a_ref[...], w_ref[...], preferred_element_type=F32)

    @pl.when(k == pl.num_programs(1) - 1)
    def _():
        out_ref[...] = _rms_rows(out_ref[...], g_ref[...])


def _ffn_out(a, w, h, g, tm, tk):
    S, D = h.shape
    d_ff = a.shape[1]
    return pl.pallas_call(
        _ffn_out_kernel,
        grid=(S // tm, d_ff // tk),
        in_specs=[pl.BlockSpec((tm, tk), lambda i, k: (i, k)),
                  pl.BlockSpec((tk, D), lambda i, k: (k, 0)),
                  pl.BlockSpec((tm, D), lambda i, k: (i, 0)),
                  pl.BlockSpec((1, D), lambda i, k: (0, 0))],
        out_specs=pl.BlockSpec((tm, D), lambda i, k: (i, 0)),
        out_shape=jax.ShapeDtypeStruct((S, D), F32),
        compiler_params=_params("arbitrary", "arbitrary"),
        name="ffn_out",
    )(a, w, h, g)


def _tiles(S, d_ff):
    return dict(
        proj_rows=min(S, 1024), proj_cols=1024, glu_cols=512,
        attn=min(S, 512),
        conv_rows=min(S, 512),
        merge_rows=min(S, 1024), merge_cols=1024,
        mix_rows=min(S, 512),
        ffn_rows=min(S, 1024), ffn_cols=512,
        out_rows=min(S, 1024), out_k=d_ff // 4,
    )


def kernel(x, norm_mix_g, w_in, lambda_q1, lambda_k1, lambda_q2, lambda_k2, subln_g, w_attn_out, dw_w, dw_b,
           conv_ln_g, conv_ln_b, w_conv_out, b_conv_out, w_mix_out, norm_ffn_g, w_ffn_in, w_ffn_out, norm_final_g):
    B, S, D = x.shape
    depth = w_in.shape[0]
    assert B == 1 and depth == 1 and D == N_HEADS * HEAD_WIDTH
    d_ff = w_ffn_out.shape[1]
    t = _tiles(S, d_ff)
    row = lambda v: v.reshape(1, -1).astype(F32)

    h0 = x.reshape(S, D)
    w_in_f = w_in[0]

    q_scale = HEAD_DIM ** -0.5 * LOG2E
    col_scale = jnp.concatenate([jnp.full((1, D), q_scale, F32), jnp.ones((1, 2 * D), F32)], axis=1)
    pc = t["proj_cols"]
    gc = t["glu_cols"]
    w_glu = w_in_f[:, 3 * D:5 * D].astype(BF16)
    y, u = _norm_glu(h0, row(norm_mix_g[0]), w_glu, 0, D // gc, t["proj_rows"], gc)
    qkv = _proj_scaled(u, w_in_f, col_scale, 3 * D, t["proj_rows"], pc)
    gates = _proj_sigmoid(u, w_in_f, 5 * D // pc, 2 * D, t["proj_rows"], pc)

    o = _attention(qkv, row(lambda_q1[0]), row(lambda_k1[0]), row(lambda_q2[0]), row(lambda_k2[0]),
                   row(subln_g[0]), t["attn"])
    z = _conv_branch(y, dw_w[0].astype(F32), row(dw_b[0]), row(conv_ln_g[0]), row(conv_ln_b[0]), t["conv_rows"])

    mixed = _merge(o, z, w_attn_out[0].astype(BF16), w_conv_out[0].astype(BF16), row(b_conv_out[0]), gates,
                   t["merge_rows"], t["merge_cols"])
    h1, f = _mixout(mixed, w_mix_out[0].astype(BF16), h0, row(norm_ffn_g[0]), t["mix_rows"])

    a = _ffn_in(f, w_ffn_in[0], t["ffn_rows"], t["ffn_cols"])
    out = _ffn_out(a, w_ffn_out[0].astype(BF16), h1, row(norm_final_g), t["out_rows"], t["out_k"])
    return out.reshape(B, S, D)
```

```python
import functools
import math

import jax
import jax.numpy as jnp
import numpy as np
from jax import lax
from jax.experimental import pallas as pl
from jax.experimental.pallas import tpu as pltpu

F32 = jnp.float32
BF16 = jnp.bfloat16

N_HEADS = 8
HEAD_DIM = 128
HEAD_WIDTH = 2 * HEAD_DIM
CHUNK = 64
CONV_KERNEL = 31
CONV_HALO = 32
NORM_EPS = 1e-5
LAMBDA_INIT = 0.8 - 0.6 * math.exp(-0.3 * 0)
LOG2E = math.log2(math.e)
LANES = 128
VMEM_LIMIT = 56 * 1024 * 1024
NEG_BIG = -1e30


def _params(*sem):
    return pltpu.CompilerParams(dimension_semantics=sem, vmem_limit_bytes=VMEM_LIMIT)


def _rms_rows(x, g):
    return x * lax.rsqrt(jnp.mean(x * x, axis=-1, keepdims=True) + NORM_EPS) * g


def _sigmoid(x):
    return 1.0 / (1.0 + jnp.exp(-x))


NORM_ROWS = 256


def _norm_into(x_ref, g_ref, u_ref):
    @pl.when(pl.program_id(1) == 0)
    def _():
        def body(r, c):
            rows = pl.ds(pl.multiple_of(r * NORM_ROWS, NORM_ROWS), NORM_ROWS)
            u_ref[rows, :] = _rms_rows(x_ref[rows, :], g_ref[...]).astype(BF16)
            return c

        lax.fori_loop(0, x_ref.shape[0] // NORM_ROWS, body, 0)


def _norm_glu_kernel(x_ref, g_ref, wa_ref, wb_ref, y_ref, u_ref):
    _norm_into(x_ref, g_ref, u_ref)
    u = u_ref[...]
    a = jnp.dot(u, wa_ref[...], preferred_element_type=F32)
    b = jnp.dot(u, wb_ref[...], preferred_element_type=F32)
    y_ref[...] = (a * _sigmoid(b)).astype(y_ref.dtype)


def _norm_glu(x, g, w, first_a, first_b, tm, tn):
    S, D = x.shape
    return pl.pallas_call(
        _norm_glu_kernel,
        grid=(S // tm, D // tn),
        in_specs=[pl.BlockSpec((tm, D), lambda i, j: (i, 0)),
                  pl.BlockSpec((1, D), lambda i, j: (0, 0)),
                  pl.BlockSpec((D, tn), lambda i, j: (0, first_a + j)),
                  pl.BlockSpec((D, tn), lambda i, j: (0, first_b + j))],
        out_specs=[pl.BlockSpec((tm, tn), lambda i, j: (i, j)),
                   pl.BlockSpec((tm, D), lambda i, j: (i, 0))],
        out_shape=[jax.ShapeDtypeStruct((S, D), BF16), jax.ShapeDtypeStruct((S, D), BF16)],
        compiler_params=_params("arbitrary", "arbitrary"),
        name="norm_glu",
    )(x, g, w, w)


def _proj_scaled_kernel(u_ref, w_ref, cs_ref, o_ref):
    r = jnp.dot(u_ref[...], w_ref[...].astype(BF16), preferred_element_type=F32)
    o_ref[...] = (r * cs_ref[...]).astype(o_ref.dtype)


def _proj_scaled(u, w, col_scale, n_out, tm, tn):
    S, D = u.shape
    return pl.pallas_call(
        _proj_scaled_kernel,
        grid=(S // tm, n_out // tn),
        in_specs=[pl.BlockSpec((tm, D), lambda i, j: (i, 0)),
                  pl.BlockSpec((D, tn), lambda i, j: (0, j)),
                  pl.BlockSpec((1, tn), lambda i, j: (0, j))],
        out_specs=pl.BlockSpec((tm, tn), lambda i, j: (i, j)),
        out_shape=jax.ShapeDtypeStruct((S, n_out), BF16),
        compiler_params=_params("arbitrary", "arbitrary"),
        name="proj_qkv",
    )(u, w, col_scale)


def _proj_sigmoid_kernel(u_ref, w_ref, o_ref):
    r = jnp.dot(u_ref[...], w_ref[...].astype(BF16), preferred_element_type=F32)
    o_ref[...] = _sigmoid(r).astype(o_ref.dtype)


def _proj_sigmoid(u, w, first_col_block, n_out, tm, tn):
    S, D = u.shape
    return pl.pallas_call(
        _proj_sigmoid_kernel,
        grid=(S // tm, n_out // tn),
        in_specs=[pl.BlockSpec((tm, D), lambda i, j: (i, 0)),
                  pl.BlockSpec((D, tn), lambda i, j: (0, first_col_block + j))],
        out_specs=pl.BlockSpec((tm, tn), lambda i, j: (i, j)),
        out_shape=jax.ShapeDtypeStruct((S, n_out), BF16),
        compiler_params=_params("arbitrary", "arbitrary"),
        name="proj_gates",
    )(u, w)


STAGE_UNROLL = 8


def _attn_kernel(slope_ref, q_ref, qn_ref, k_ref, v_ref, qaug_ref, kaug_ref, dx_ref, lq1_ref, lk1_ref, lq2_ref,
                 lk2_ref, g_ref, o_ref, m_ref, l_ref, acc_ref, sa_ref, xa_ref, sb_ref, xb_ref, sc_ref, xc_ref, *, t):
    h = pl.program_id(0)
    i = pl.program_id(1)
    slope = slope_ref[h]
    shift = slope * t
    reps = t // LANES

    q_extra = jnp.broadcast_to(qaug_ref[...], (t, HEAD_DIM)).astype(BF16)

    def augmented(ref):
        return tuple(jnp.concatenate([ref[:, j * HEAD_DIM:(j + 1) * HEAD_DIM], q_extra], axis=1) for j in range(2))

    q = augmented(q_ref)

    def block_rows(blk):
        return pl.ds(pl.multiple_of(blk * t, t), t)

    def scores(blk, buf, q_maps=q):
        s_ref, x_ref = buf
        rows = block_rows(blk)
        for j in range(2):
            k_j = jnp.concatenate([k_ref[rows, j * HEAD_DIM:(j + 1) * HEAD_DIM], kaug_ref[...]], axis=1)
            s = lax.dot_general(q_maps[j], k_j, (((1,), (1,)), ((), ())), preferred_element_type=F32)
            s_ref[j] = s
            lane_max = s[:, :LANES]
            for r in range(1, reps):
                lane_max = jnp.maximum(lane_max, s[:, r * LANES:(r + 1) * LANES])
            x_ref[j] = lane_max

    def softmax_pv(blk, buf, diagonal, first=False):
        s_ref, x_ref = buf
        ps, alphas = [], []
        for j in range(2):
            if diagonal:
                s = s_ref[j] + slope * dx_ref[...]
                m_cur = jnp.max(s, axis=1, keepdims=True)
            else:
                s = s_ref[j]
                m_cur = jnp.max(x_ref[j], axis=1, keepdims=True)
            if first:
                m_new = jnp.broadcast_to(m_cur, (t, LANES))
            else:
                m_prev = m_ref[j] - shift
                m_new = jnp.maximum(m_prev, m_cur)
                alpha = jnp.exp2(m_prev - m_new)
                alphas.append(jnp.concatenate([alpha] * (HEAD_WIDTH // LANES), axis=1))
            p = jnp.exp2(s - jnp.concatenate([m_new] * reps, axis=1))
            lane_sums = p[:, :LANES]
            for r in range(1, reps):
                lane_sums = lane_sums + p[:, r * LANES:(r + 1) * LANES]
            l_ref[j] = lane_sums if first else alpha * l_ref[j] + lane_sums
            m_ref[j] = m_new
            ps.append(p.astype(BF16))
        pv = jnp.dot(jnp.concatenate(ps, axis=0), v_ref[block_rows(blk), :], preferred_element_type=F32)
        acc_ref[...] = pv if first else acc_ref[...] * jnp.concatenate(alphas, axis=0) + pv

    buf_a = (sa_ref, xa_ref)
    buf_b = (sb_ref, xb_ref)
    buf_c = (sc_ref, xc_ref)

    def stage(blk, cur, nxt):
        scores(blk + 1, nxt)
        softmax_pv(blk, cur, diagonal=False)

    def stages(first_blk, count):
        for d in range(0, count, 2):
            stage(first_blk + d, buf_a, buf_b)
            stage(first_blk + d + 1, buf_b, buf_a)

    def diagonal_and_next(buf, first=False):
        softmax_pv(i, buf, diagonal=True, first=first)
        scores(0, buf_c, augmented(qn_ref))

    @pl.when(i == 0)
    def _():
        scores(0, buf_c)
        diagonal_and_next(buf_c, first=True)

    @pl.when(i > 0)
    def _():
        scores(1, buf_a)
        softmax_pv(0, buf_c, diagonal=False, first=True)

    rest = jnp.maximum(i - 1, 0)

    def unrolled(u, c):
        stages(1 + STAGE_UNROLL * u, STAGE_UNROLL)
        return c

    lax.fori_loop(0, rest // STAGE_UNROLL, unrolled, 0)
    piece = STAGE_UNROLL // 2
    while piece >= 2:
        @pl.when((rest & piece) != 0)
        def _(piece=piece):
            stages(1 + (rest & ~(2 * piece - 1)), piece)

        piece //= 2

    odd = rest % 2 == 1

    @pl.when(jnp.logical_and(i > 0, odd))
    def _():
        stage(i - 1, buf_a, buf_b)
        diagonal_and_next(buf_b)

    @pl.when(jnp.logical_and(i > 0, jnp.logical_not(odd)))
    def _():
        diagonal_and_next(buf_a)

    lam = (jnp.exp(jnp.sum(lq1_ref[...] * lk1_ref[...], axis=1, keepdims=True))
           - jnp.exp(jnp.sum(lq2_ref[...] * lk2_ref[...], axis=1, keepdims=True)) + LAMBDA_INIT)
    o1 = acc_ref[:t, :] / jnp.sum(l_ref[0], axis=1, keepdims=True)
    o2 = acc_ref[t:, :] / jnp.sum(l_ref[1], axis=1, keepdims=True)
    o = o1 - lam * o2
    o_ref[...] = (_rms_rows(o, g_ref[...]) * (1.0 - LAMBDA_INIT)).astype(o_ref.dtype)


BIAS_SPLIT = 256
LOG2E_PIECES = 3


def _diag_extra(t):
    r = lax.broadcasted_iota(jnp.int32, (t, t), 0)
    c = lax.broadcasted_iota(jnp.int32, (t, t), 1)
    allowed = (c // CHUNK) <= (r // CHUNK)
    return jnp.where(allowed, (-2 * jnp.maximum(c - r, 0)).astype(F32), -jnp.inf)


def _bias_columns(t):
    pieces, rest = [], LOG2E
    for _ in range(LOG2E_PIECES):
        p = float(np.asarray(rest, np.float32).astype(BF16).astype(np.float32))
        pieces.append(p)
        rest -= p
    q_cols = jnp.zeros((1, HEAD_DIM), F32).at[0, :2 * LOG2E_PIECES].set(jnp.asarray(pieces * 2, F32))
    slopes = jnp.exp2(-(jnp.arange(N_HEADS, dtype=F32) + 1.0) * (8.0 / N_HEADS))
    c = jnp.arange(t, dtype=jnp.int32)
    lo = (c % BIAS_SPLIT).astype(F32)
    hi = (c - c % BIAS_SPLIT).astype(F32)
    k_cols = jnp.concatenate([jnp.tile(lo[:, None], (1, LOG2E_PIECES)), jnp.tile(hi[:, None], (1, LOG2E_PIECES)),
                              jnp.zeros((t, HEAD_DIM - 2 * LOG2E_PIECES), F32)], axis=1)
    k_cols = (slopes[:, None, None] * k_cols[None]).astype(BF16)
    return q_cols, k_cols, slopes * LOG2E


def _attention(qkv, lq1, lk1, lq2, lk2, subln_g, t):
    S = qkv.shape[0]
    n_tiles = S // t
    q_cols, k_cols, slopes = _bias_columns(t)
    vec = pl.BlockSpec((1, HEAD_DIM), lambda h, i: (0, 0))
    return pl.pallas_call(
        functools.partial(_attn_kernel, t=t),
        grid=(N_HEADS, n_tiles),
        in_specs=[pl.BlockSpec(memory_space=pltpu.SMEM),
                  pl.BlockSpec((t, HEAD_WIDTH), lambda h, i: (i, h)),
                  pl.BlockSpec((t, HEAD_WIDTH), lambda h, i: (jnp.minimum(i + 1, n_tiles - 1), h)),
                  pl.BlockSpec((S, HEAD_WIDTH), lambda h, i: (0, N_HEADS + h)),
                  pl.BlockSpec((S, HEAD_WIDTH), lambda h, i: (0, 2 * N_HEADS + h)),
                  vec,
                  pl.BlockSpec((None, t, HEAD_DIM), lambda h, i: (h, 0, 0)),
                  pl.BlockSpec((t, t), lambda h, i: (0, 0)),
                  vec, vec, vec, vec,
                  pl.BlockSpec((1, HEAD_WIDTH), lambda h, i: (0, 0))],
        out_specs=pl.BlockSpec((t, HEAD_WIDTH), lambda h, i: (i, h)),
        out_shape=jax.ShapeDtypeStruct((S, N_HEADS * HEAD_WIDTH), BF16),
        scratch_shapes=[pltpu.VMEM((2, t, LANES), F32),
                        pltpu.VMEM((2, t, LANES), F32),
                        pltpu.VMEM((2 * t, HEAD_WIDTH), F32)] + 3 * [
                            pltpu.VMEM((2, t, t), F32),
                            pltpu.VMEM((2, t, LANES), F32)],
        compiler_params=_params("arbitrary", "arbitrary"),
        name="attention",
    )(slopes, qkv, qkv, qkv, qkv, q_cols, k_cols, _diag_extra(t), lq1, lk1, lq2, lk2, subln_g)


CONV_ROWS = 128
CONV_COLS = 256
SUBLANES = 8


def _conv_kernel(yp_ref, y_ref, w_ref, b_ref, lg_ref, lb_ref, z_ref, ext_ref, xs_ref, cv_ref):
    tm = y_ref.shape[0]
    cb = pl.program_id(1)
    n_cb = pl.num_programs(1)
    first = pl.program_id(0) == 0
    ext_ref[:CONV_HALO, :] = jnp.where(first, 0.0, yp_ref[...].astype(F32))
    ext_ref[CONV_HALO:, :] = y_ref[...].astype(F32)
    shifted_rows = xs_ref.shape[1]
    for r in range(1, SUBLANES):
        xs_ref[r - 1] = ext_ref[r:r + shifted_rows, :]
    lead = CONV_HALO - (CONV_KERNEL - 1)
    for r0 in range(0, tm, CONV_ROWS):
        acc = jnp.broadcast_to(b_ref[...], (CONV_ROWS, CONV_COLS))
        for j in range(CONV_KERNEL):
            r, a = (lead + j) % SUBLANES, (lead + j) // SUBLANES
            src = ext_ref if r == 0 else xs_ref.at[r - 1]
            lo = r0 + SUBLANES * a
            acc = acc + src[lo:lo + CONV_ROWS, :] * w_ref[j:j + 1, :]
        cv_ref[cb, r0:r0 + CONV_ROWS, :] = acc

    @pl.when(cb == n_cb - 1)
    def _():
        n_blocks = cv_ref.shape[0]
        n_ch = n_blocks * CONV_COLS

        def rows_body(rc, c):
            rows = pl.ds(pl.multiple_of(rc * CONV_ROWS, CONV_ROWS), CONV_ROWS)
            blocks = [cv_ref[k, rows, :] for k in range(n_blocks)]
            mu = sum(jnp.sum(b, axis=1, keepdims=True) for b in blocks) / n_ch
            var = sum(jnp.sum((b - mu) * (b - mu), axis=1, keepdims=True) for b in blocks) / n_ch
            inv = lax.rsqrt(var + NORM_EPS)
            for k, b in enumerate(blocks):
                cols = slice(k * CONV_COLS, (k + 1) * CONV_COLS)
                zn = (b - mu) * inv * lg_ref[:, cols] + lb_ref[:, cols]
                z_ref[rows, cols] = (zn * _sigmoid(zn)).astype(z_ref.dtype)
            return c

        lax.fori_loop(0, tm // CONV_ROWS, rows_body, 0)


def _conv_branch(y, dw_w, dw_b, ln_g, ln_b, tm):
    S, C = y.shape
    halo_blocks = tm // CONV_HALO
    n_cb = C // CONV_COLS
    row = pl.BlockSpec((1, C), lambda i, c: (0, 0))
    return pl.pallas_call(
        _conv_kernel,
        grid=(S // tm, n_cb),
        in_specs=[pl.BlockSpec((CONV_HALO, CONV_COLS), lambda i, c: (jnp.maximum(i * halo_blocks - 1, 0), c)),
                  pl.BlockSpec((tm, CONV_COLS), lambda i, c: (i, c)),
                  pl.BlockSpec((CONV_KERNEL, CONV_COLS), lambda i, c: (0, c)),
                  pl.BlockSpec((1, CONV_COLS), lambda i, c: (0, c)),
                  row, row],
        out_specs=pl.BlockSpec((tm, C), lambda i, c: (i, 0)),
        out_shape=jax.ShapeDtypeStruct((S, C), BF16),
        scratch_shapes=[pltpu.VMEM((tm + CONV_HALO, CONV_COLS), F32),
                        pltpu.VMEM((SUBLANES - 1, tm + CONV_HALO - SUBLANES, CONV_COLS), F32),
                        pltpu.VMEM((n_cb, tm, CONV_COLS), F32)],
        compiler_params=_params("arbitrary", "arbitrary"),
        name="conv_branch",
    )(y, y, dw_w, dw_b, ln_g, ln_b)


def _merge_kernel(o_ref, z_ref, wa_ref, wc_ref, bc_ref, ga_ref, gc_ref, out_ref):
    a = jnp.dot(o_ref[...], wa_ref[...], preferred_element_type=F32)
    c = jnp.dot(z_ref[...], wc_ref[...], preferred_element_type=F32) + bc_ref[...]
    out_ref[...] = (ga_ref[...].astype(F32) * a + gc_ref[...].astype(F32) * c).astype(out_ref.dtype)


def _merge(o, z, wa, wc, bc, gates, tm, tn):
    S, D = o.shape
    nb = D // tn
    return pl.pallas_call(
        _merge_kernel,
        grid=(S // tm, nb),
        in_specs=[pl.BlockSpec((tm, D), lambda i, j: (i, 0)),
                  pl.BlockSpec((tm, D), lambda i, j: (i, 0)),
                  pl.BlockSpec((D, tn), lambda i, j: (0, j)),
                  pl.BlockSpec((D, tn), lambda i, j: (0, j)),
                  pl.BlockSpec((1, tn), lambda i, j: (0, j)),
                  pl.BlockSpec((tm, tn), lambda i, j: (i, j)),
                  pl.BlockSpec((tm, tn), lambda i, j: (i, nb + j))],
        out_specs=pl.BlockSpec((tm, tn), lambda i, j: (i, j)),
        out_shape=jax.ShapeDtypeStruct((S, D), BF16),
        compiler_params=_params("arbitrary", "arbitrary"),
        name="merge",
    )(o, z, wa, wc, bc, gates, gates)


def _mixout_kernel(m_ref, w_ref, x_ref, g_ref, h_ref, f_ref):
    h = x_ref[...] + jnp.dot(m_ref[...], w_ref[...], preferred_element_type=F32)
    h_ref[...] = h
    f_ref[...] = _rms_rows(h, g_ref[...]).astype(f_ref.dtype)


def _mixout(mixed, w, x, g, tm):
    S, D = x.shape
    tile = pl.BlockSpec((tm, D), lambda i: (i, 0))
    return pl.pallas_call(
        _mixout_kernel,
        grid=(S // tm,),
        in_specs=[tile,
                  pl.BlockSpec((D, D), lambda i: (0, 0)),
                  tile,
                  pl.BlockSpec((1, D), lambda i: (0, 0))],
        out_specs=[tile, tile],
        out_shape=[jax.ShapeDtypeStruct((S, D), F32), jax.ShapeDtypeStruct((S, D), BF16)],
        compiler_params=_params("arbitrary"),
        name="mixout",
    )(mixed, w, x, g)


def _ffn_in_kernel(f_ref, wg_ref, wu_ref, a_ref):
    f = f_ref[...]
    g = jnp.dot(f, wg_ref[...].astype(BF16), preferred_element_type=F32)
    u = jnp.dot(f, wu_ref[...].astype(BF16), preferred_element_type=F32)
    a_ref[...] = (g * _sigmoid(g) * u).astype(a_ref.dtype)


def _ffn_in(f, w, tm, tn):
    S, D = f.shape
    d_ff = w.shape[1] // 2
    nb = d_ff // tn
    return pl.pallas_call(
        _ffn_in_kernel,
        grid=(S // tm, nb),
        in_specs=[pl.BlockSpec((tm, D), lambda i, j: (i, 0)),
                  pl.BlockSpec((D, tn), lambda i, j: (0, j)),
                  pl.BlockSpec((D, tn), lambda i, j: (0, nb + j))],
        out_specs=pl.BlockSpec((tm, tn), lambda i, j: (i, j)),
        out_shape=jax.ShapeDtypeStruct((S, d_ff), BF16),
        compiler_params=_params("arbitrary", "arbitrary"),
        name="ffn_in",
    )(f, w, w)


def _ffn_out_kernel(a_ref, w_ref, h_ref, g_ref, out_ref):
    k = pl.program_id(1)
    last = pl.num_programs(1) - 1

    def partial_product():
        return jnp.dot(a_ref[...], w_ref[...], preferred_element_type=F32)

    @pl.when(k == 0)
    def _():
        out_ref[...] = h_ref[...] + partial_product()

    @pl.when(jnp.logical_and(k > 0, k < last))
    def _():
        out_ref[...] += partial_product()

    @pl.when(k == last)
    def _():
        out_ref[...] = _rms_rows(out_ref[...] + partial_product(), g_ref[...])


def _ffn_out(a, w, h, g, tm, tk):
    S, D = h.shape
    d_ff = a.shape[1]
    assert d_ff // tk >= 2
    return pl.pallas_call(
        _ffn_out_kernel,
        grid=(S // tm, d_ff // tk),
        in_specs=[pl.BlockSpec((tm, tk), lambda i, k: (i, k)),
                  pl.BlockSpec((tk, D), lambda i, k: (k, 0)),
                  pl.BlockSpec((tm, D), lambda i, k: (i, 0)),
                  pl.BlockSpec((1, D), lambda i, k: (0, 0))],
        out_specs=pl.BlockSpec((tm, D), lambda i, k: (i, 0)),
        out_shape=jax.ShapeDtypeStruct((S, D), F32),
        compiler_params=_params("arbitrary", "arbitrary"),
        name="ffn_out",
    )(a, w, h, g)


def _tiles(S, d_ff):
    return dict(
        proj_rows=min(S, 1024), proj_cols=1024, glu_cols=512,
        attn=min(S, 512),
        conv_rows=min(S, 512),
        merge_rows=min(S, 1024), merge_cols=1024,
        mix_rows=min(S, 512),
        ffn_rows=min(S, 1024), ffn_cols=512,
        out_rows=min(S, 1024), out_k=d_ff // 4,
    )


def kernel(x, norm_mix_g, w_in, lambda_q1, lambda_k1, lambda_q2, lambda_k2, subln_g, w_attn_out, dw_w, dw_b,
           conv_ln_g, conv_ln_b, w_conv_out, b_conv_out, w_mix_out, norm_ffn_g, w_ffn_in, w_ffn_out, norm_final_g):
    B, S, D = x.shape
    depth = w_in.shape[0]
    assert B == 1 and depth == 1 and D == N_HEADS * HEAD_WIDTH
    d_ff = w_ffn_out.shape[1]
    t = _tiles(S, d_ff)
    row = lambda v: v.reshape(1, -1).astype(F32)

    h0 = x.reshape(S, D)
    w_in_f = w_in[0]

    q_scale = HEAD_DIM ** -0.5 * LOG2E
    col_scale = jnp.concatenate([jnp.full((1, D), q_scale, F32), jnp.ones((1, 2 * D), F32)], axis=1)
    pc = t["proj_cols"]
    gc = t["glu_cols"]
    w_glu = w_in_f[:, 3 * D:5 * D].astype(BF16)
    y, u = _norm_glu(h0, row(norm_mix_g[0]), w_glu, 0, D // gc, t["proj_rows"], gc)
    qkv = _proj_scaled(u, w_in_f, col_scale, 3 * D, t["proj_rows"], pc)
    gates = _proj_sigmoid(u, w_in_f, 5 * D // pc, 2 * D, t["proj_rows"], pc)

    o = _attention(qkv, row(lambda_q1[0]), row(lambda_k1[0]), row(lambda_q2[0]), row(lambda_k2[0]),
                   row(subln_g[0]), t["attn"])
    z = _conv_branch(y, dw_w[0].astype(F32), row(dw_b[0]), row(conv_ln_g[0]), row(conv_ln_b[0]), t["conv_rows"])

    mixed = _merge(o, z, w_attn_out[0].astype(BF16), w_conv_out[0].astype(BF16), row(b_conv_out[0]), gates,
                   t["merge_rows"], t["merge_cols"])
    h1, f = _mixout(mixed, w_mix_out[0].astype(BF16), h0, row(norm_ffn_g[0]), t["mix_rows"])

    a = _ffn_in(f, w_ffn_in[0], t["ffn_rows"], t["ffn_cols"])
    out = _ffn_out(a, w_ffn_out[0].astype(BF16), h1, row(norm_final_g), t["out_rows"], t["out_k"])
    return out.reshape(B, S, D)
```

```python
import functools
import math

import jax
import jax.numpy as jnp
import numpy as np
from jax import lax
from jax.experimental import pallas as pl
from jax.experimental.pallas import tpu as pltpu

F32 = jnp.float32
BF16 = jnp.bfloat16

N_HEADS = 8
HEAD_DIM = 128
HEAD_WIDTH = 2 * HEAD_DIM
CHUNK = 64
CONV_KERNEL = 31
CONV_HALO = 32
NORM_EPS = 1e-5
LAMBDA_INIT = 0.8 - 0.6 * math.exp(-0.3 * 0)
LOG2E = math.log2(math.e)
LANES = 128
VMEM_LIMIT = 56 * 1024 * 1024
NEG_BIG = -1e30


def _params(*sem):
    return pltpu.CompilerParams(dimension_semantics=sem, vmem_limit_bytes=VMEM_LIMIT)


def _rms_rows(x, g):
    return x * lax.rsqrt(jnp.mean(x * x, axis=-1, keepdims=True) + NORM_EPS) * g


def _sigmoid(x):
    return 1.0 / (1.0 + jnp.exp(-x))


NORM_ROWS = 256


def _norm_into(x_ref, g_ref, u_ref):
    @pl.when(pl.program_id(1) == 0)
    def _():
        def body(r, c):
            rows = pl.ds(pl.multiple_of(r * NORM_ROWS, NORM_ROWS), NORM_ROWS)
            u_ref[rows, :] = _rms_rows(x_ref[rows, :], g_ref[...]).astype(BF16)
            return c

        lax.fori_loop(0, x_ref.shape[0] // NORM_ROWS, body, 0)


def _norm_glu_kernel(x_ref, g_ref, wa_ref, wb_ref, y_ref, u_ref):
    _norm_into(x_ref, g_ref, u_ref)
    u = u_ref[...]
    gate = _sigmoid(jnp.dot(u, wb_ref[...], preferred_element_type=F32))
    a = jnp.dot(u, wa_ref[...], preferred_element_type=F32)
    y_ref[...] = (a * gate).astype(y_ref.dtype)


def _norm_glu(x, g, w, first_a, first_b, tm, tn):
    S, D = x.shape
    return pl.pallas_call(
        _norm_glu_kernel,
        grid=(S // tm, D // tn),
        in_specs=[pl.BlockSpec((tm, D), lambda i, j: (i, 0)),
                  pl.BlockSpec((1, D), lambda i, j: (0, 0)),
                  pl.BlockSpec((D, tn), lambda i, j: (0, first_a + j)),
                  pl.BlockSpec((D, tn), lambda i, j: (0, first_b + j))],
        out_specs=[pl.BlockSpec((tm, tn), lambda i, j: (i, j)),
                   pl.BlockSpec((tm, D), lambda i, j: (i, 0))],
        out_shape=[jax.ShapeDtypeStruct((S, D), BF16), jax.ShapeDtypeStruct((S, D), BF16)],
        compiler_params=_params("arbitrary", "arbitrary"),
        name="norm_glu",
    )(x, g, w, w)


def _proj_scaled_kernel(u_ref, w_ref, cs_ref, o_ref):
    r = jnp.dot(u_ref[...], w_ref[...].astype(BF16), preferred_element_type=F32)
    o_ref[...] = (r * cs_ref[...]).astype(o_ref.dtype)


def _proj_scaled(u, w, col_scale, n_out, tm, tn):
    S, D = u.shape
    return pl.pallas_call(
        _proj_scaled_kernel,
        grid=(S // tm, n_out // tn),
        in_specs=[pl.BlockSpec((tm, D), lambda i, j: (i, 0)),
                  pl.BlockSpec((D, tn), lambda i, j: (0, j)),
                  pl.BlockSpec((1, tn), lambda i, j: (0, j))],
        out_specs=pl.BlockSpec((tm, tn), lambda i, j: (i, j)),
        out_shape=jax.ShapeDtypeStruct((S, n_out), BF16),
        compiler_params=_params("arbitrary", "arbitrary"),
        name="proj_qkv",
    )(u, w, col_scale)


def _proj_sigmoid_kernel(u_ref, w_ref, o_ref):
    r = jnp.dot(u_ref[...], w_ref[...].astype(BF16), preferred_element_type=F32)
    o_ref[...] = _sigmoid(r).astype(o_ref.dtype)


def _proj_sigmoid(u, w, first_col_block, n_out, tm, tn):
    S, D = u.shape
    return pl.pallas_call(
        _proj_sigmoid_kernel,
        grid=(S // tm, n_out // tn),
        in_specs=[pl.BlockSpec((tm, D), lambda i, j: (i, 0)),
                  pl.BlockSpec((D, tn), lambda i, j: (0, first_col_block + j))],
        out_specs=pl.BlockSpec((tm, tn), lambda i, j: (i, j)),
        out_shape=jax.ShapeDtypeStruct((S, n_out), BF16),
        compiler_params=_params("arbitrary", "arbitrary"),
        name="proj_gates",
    )(u, w)


STAGE_UNROLL = 4


def _attn_kernel(slope_ref, q_ref, qn_ref, k_ref, v_ref, qaug_ref, kaug_ref, dx_ref, lq1_ref, lk1_ref, lq2_ref,
                 lk2_ref, g_ref, o_ref, m_ref, l_ref, acc_ref, sa_ref, xa_ref, sb_ref, xb_ref, sc_ref, xc_ref, *, t):
    w = 2 * t
    h = pl.program_id(0)
    i = pl.program_id(1)
    last_pair = i // 2
    tile_is_odd = i % 2 == 1
    slope = slope_ref[h]
    shift = slope * w
    reps = w // LANES

    q_extra = jnp.broadcast_to(qaug_ref[...], (t, HEAD_DIM)).astype(BF16)

    def augmented(ref):
        return tuple(jnp.concatenate([ref[:, j * HEAD_DIM:(j + 1) * HEAD_DIM], q_extra], axis=1) for j in range(2))

    q = augmented(q_ref)

    def pair_rows(pair):
        return pl.ds(pl.multiple_of(pair * w, w), w)

    def scores(pair, buf, q_maps=q):
        s_ref, x_ref = buf
        rows = pair_rows(pair)
        for j in range(2):
            k_j = jnp.concatenate([k_ref[rows, j * HEAD_DIM:(j + 1) * HEAD_DIM], kaug_ref[...]], axis=1)
            s = lax.dot_general(q_maps[j], k_j, (((1,), (1,)), ((), ())), preferred_element_type=F32)
            s_ref[j] = s
            lane_max = s[:, :LANES]
            for r in range(1, reps):
                lane_max = jnp.maximum(lane_max, s[:, r * LANES:(r + 1) * LANES])
            x_ref[j] = lane_max

    def diagonal_bias():
        dx = dx_ref[...]
        left = jnp.where(tile_is_odd, 0.0, dx)
        right = jnp.where(tile_is_odd, dx, -jnp.inf)
        return slope * jnp.concatenate([left, right], axis=1)

    def softmax_pv(pair, buf, diagonal, first=False):
        s_ref, x_ref = buf
        ps, alphas = [], []
        bias = diagonal_bias() if diagonal else None
        for j in range(2):
            if diagonal:
                s = s_ref[j] + bias
                m_cur = jnp.max(s, axis=1, keepdims=True)
            else:
                s = s_ref[j]
                m_cur = jnp.max(x_ref[j], axis=1, keepdims=True)
            if first:
                m_new = jnp.broadcast_to(m_cur, (t, LANES))
            else:
                m_prev = m_ref[j] - shift
                m_new = jnp.maximum(m_prev, m_cur)
                alpha = jnp.exp2(m_prev - m_new)
                alphas.append(jnp.concatenate([alpha] * (HEAD_WIDTH // LANES), axis=1))
            p = jnp.exp2(s - jnp.concatenate([m_new] * reps, axis=1))
            lane_sums = p[:, :LANES]
            for r in range(1, reps):
                lane_sums = lane_sums + p[:, r * LANES:(r + 1) * LANES]
            l_ref[j] = lane_sums if first else alpha * l_ref[j] + lane_sums
            m_ref[j] = m_new
            ps.append(p.astype(BF16))
        pv = jnp.dot(jnp.concatenate(ps, axis=0), v_ref[pair_rows(pair), :], preferred_element_type=F32)
        acc_ref[...] = pv if first else acc_ref[...] * jnp.concatenate(alphas, axis=0) + pv

    buf_a = (sa_ref, xa_ref)
    buf_b = (sb_ref, xb_ref)
    buf_c = (sc_ref, xc_ref)

    def stage(pair, cur, nxt):
        scores(pair + 1, nxt)
        softmax_pv(pair, cur, diagonal=False)

    def stages(first_pair, count):
        for d in range(0, count, 2):
            stage(first_pair + d, buf_a, buf_b)
            stage(first_pair + d + 1, buf_b, buf_a)

    def diagonal_and_next(buf, first=False):
        softmax_pv(last_pair, buf, diagonal=True, first=first)
        scores(0, buf_c, augmented(qn_ref))

    @pl.when(i == 0)
    def _():
        scores(0, buf_c)

    @pl.when(last_pair == 0)
    def _():
        diagonal_and_next(buf_c, first=True)

    @pl.when(last_pair > 0)
    def _():
        scores(1, buf_a)
        softmax_pv(0, buf_c, diagonal=False, first=True)

    rest = jnp.maximum(last_pair - 1, 0)

    def unrolled(u, c):
        stages(1 + STAGE_UNROLL * u, STAGE_UNROLL)
        return c

    lax.fori_loop(0, rest // STAGE_UNROLL, unrolled, 0)
    piece = STAGE_UNROLL // 2
    while piece >= 2:
        @pl.when((rest & piece) != 0)
        def _(piece=piece):
            stages(1 + (rest & ~(2 * piece - 1)), piece)

        piece //= 2

    odd = rest % 2 == 1

    @pl.when(jnp.logical_and(last_pair > 0, odd))
    def _():
        stage(last_pair - 1, buf_a, buf_b)
        diagonal_and_next(buf_b)

    @pl.when(jnp.logical_and(last_pair > 0, jnp.logical_not(odd)))
    def _():
        diagonal_and_next(buf_a)

    lam = (jnp.exp(jnp.sum(lq1_ref[...] * lk1_ref[...], axis=1, keepdims=True))
           - jnp.exp(jnp.sum(lq2_ref[...] * lk2_ref[...], axis=1, keepdims=True)) + LAMBDA_INIT)
    o1 = acc_ref[:t, :] / jnp.sum(l_ref[0], axis=1, keepdims=True)
    o2 = acc_ref[t:, :] / jnp.sum(l_ref[1], axis=1, keepdims=True)
    o = o1 - lam * o2
    o_ref[...] = (_rms_rows(o, g_ref[...]) * (1.0 - LAMBDA_INIT)).astype(o_ref.dtype)


BIAS_SPLIT = 256
LOG2E_PIECES = 3


def _diag_extra(t):
    r = lax.broadcasted_iota(jnp.int32, (t, t), 0)
    c = lax.broadcasted_iota(jnp.int32, (t, t), 1)
    allowed = (c // CHUNK) <= (r // CHUNK)
    return jnp.where(allowed, (-2 * jnp.maximum(c - r, 0)).astype(F32), -jnp.inf)


def _bias_columns(w):
    pieces, rest = [], LOG2E
    for _ in range(LOG2E_PIECES):
        p = float(np.asarray(rest, np.float32).astype(BF16).astype(np.float32))
        pieces.append(p)
        rest -= p
    q_cols = jnp.zeros((1, HEAD_DIM), F32).at[0, :2 * LOG2E_PIECES].set(jnp.asarray(pieces * 2, F32))
    slopes = jnp.exp2(-(jnp.arange(N_HEADS, dtype=F32) + 1.0) * (8.0 / N_HEADS))
    c = jnp.arange(w, dtype=jnp.int32)
    lo = (c % BIAS_SPLIT).astype(F32)
    hi = (c - c % BIAS_SPLIT).astype(F32)
    k_cols = jnp.concatenate([jnp.tile(lo[:, None], (1, LOG2E_PIECES)), jnp.tile(hi[:, None], (1, LOG2E_PIECES)),
                              jnp.zeros((w, HEAD_DIM - 2 * LOG2E_PIECES), F32)], axis=1)
    k_cols = (slopes[:, None, None] * k_cols[None]).astype(BF16)
    return q_cols, k_cols, slopes * LOG2E


def _attention(qkv, lq1, lk1, lq2, lk2, subln_g, t):
    S = qkv.shape[0]
    n_tiles = S // t
    w = 2 * t
    assert n_tiles % 2 == 0
    q_cols, k_cols, slopes = _bias_columns(w)
    vec = pl.BlockSpec((1, HEAD_DIM), lambda h, i: (0, 0))
    return pl.pallas_call(
        functools.partial(_attn_kernel, t=t),
        grid=(N_HEADS, n_tiles),
        in_specs=[pl.BlockSpec(memory_space=pltpu.SMEM),
                  pl.BlockSpec((t, HEAD_WIDTH), lambda h, i: (i, h)),
                  pl.BlockSpec((t, HEAD_WIDTH), lambda h, i: (jnp.minimum(i + 1, n_tiles - 1), h)),
                  pl.BlockSpec((S, HEAD_WIDTH), lambda h, i: (0, N_HEADS + h)),
                  pl.BlockSpec((S, HEAD_WIDTH), lambda h, i: (0, 2 * N_HEADS + h)),
                  vec,
                  pl.BlockSpec((None, w, HEAD_DIM), lambda h, i: (h, 0, 0)),
                  pl.BlockSpec((t, t), lambda h, i: (0, 0)),
                  vec, vec, vec, vec,
                  pl.BlockSpec((1, HEAD_WIDTH), lambda h, i: (0, 0))],
        out_specs=pl.BlockSpec((t, HEAD_WIDTH), lambda h, i: (i, h)),
        out_shape=jax.ShapeDtypeStruct((S, N_HEADS * HEAD_WIDTH), BF16),
        scratch_shapes=[pltpu.VMEM((2, t, LANES), F32),
                        pltpu.VMEM((2, t, LANES), F32),
                        pltpu.VMEM((2 * t, HEAD_WIDTH), F32)] + 3 * [
                            pltpu.VMEM((2, t, w), F32),
                            pltpu.VMEM((2, t, LANES), F32)],
        compiler_params=_params("arbitrary", "arbitrary"),
        name="attention",
    )(slopes, qkv, qkv, qkv, qkv, q_cols, k_cols, _diag_extra(t), lq1, lk1, lq2, lk2, subln_g)


CONV_ROWS = 128
CONV_COLS = 256
SUBLANES = 8


def _conv_kernel(yp_ref, y_ref, w_ref, b_ref, lg_ref, lb_ref, z_ref, ext_ref, xs_ref, cv_ref):
    tm = y_ref.shape[0]
    cb = pl.program_id(1)
    n_cb = pl.num_programs(1)
    first = pl.program_id(0) == 0
    ext_ref[:CONV_HALO, :] = jnp.where(first, 0.0, yp_ref[...].astype(F32))
    ext_ref[CONV_HALO:, :] = y_ref[...].astype(F32)
    shifted_rows = xs_ref.shape[1]
    for r in range(1, SUBLANES):
        xs_ref[r - 1] = ext_ref[r:r + shifted_rows, :]
    lead = CONV_HALO - (CONV_KERNEL - 1)
    for r0 in range(0, tm, CONV_ROWS):
        acc = jnp.broadcast_to(b_ref[...], (CONV_ROWS, CONV_COLS))
        for j in range(CONV_KERNEL):
            r, a = (lead + j) % SUBLANES, (lead + j) // SUBLANES
            src = ext_ref if r == 0 else xs_ref.at[r - 1]
            lo = r0 + SUBLANES * a
            acc = acc + src[lo:lo + CONV_ROWS, :] * w_ref[j:j + 1, :]
        cv_ref[cb, r0:r0 + CONV_ROWS, :] = acc

    @pl.when(cb == n_cb - 1)
    def _():
        n_blocks = cv_ref.shape[0]
        n_ch = n_blocks * CONV_COLS

        def rows_body(rc, c):
            rows = pl.ds(pl.multiple_of(rc * CONV_ROWS, CONV_ROWS), CONV_ROWS)
            blocks = [cv_ref[k, rows, :] for k in range(n_blocks)]
            mu = sum(jnp.sum(b, axis=1, keepdims=True) for b in blocks) / n_ch
            var = sum(jnp.sum((b - mu) * (b - mu), axis=1, keepdims=True) for b in blocks) / n_ch
            inv = lax.rsqrt(var + NORM_EPS)
            for k, b in enumerate(blocks):
                cols = slice(k * CONV_COLS, (k + 1) * CONV_COLS)
                zn = (b - mu) * inv * lg_ref[:, cols] + lb_ref[:, cols]
                z_ref[rows, cols] = (zn * _sigmoid(zn)).astype(z_ref.dtype)
            return c

        lax.fori_loop(0, tm // CONV_ROWS, rows_body, 0)


def _conv_branch(y, dw_w, dw_b, ln_g, ln_b, tm):
    S, C = y.shape
    halo_blocks = tm // CONV_HALO
    n_cb = C // CONV_COLS
    row = pl.BlockSpec((1, C), lambda i, c: (0, 0))
    return pl.pallas_call(
        _conv_kernel,
        grid=(S // tm, n_cb),
        in_specs=[pl.BlockSpec((CONV_HALO, CONV_COLS), lambda i, c: (jnp.maximum(i * halo_blocks - 1, 0), c)),
                  pl.BlockSpec((tm, CONV_COLS), lambda i, c: (i, c)),
                  pl.BlockSpec((CONV_KERNEL, CONV_COLS), lambda i, c: (0, c)),
                  pl.BlockSpec((1, CONV_COLS), lambda i, c: (0, c)),
                  row, row],
        out_specs=pl.BlockSpec((tm, C), lambda i, c: (i, 0)),
        out_shape=jax.ShapeDtypeStruct((S, C), BF16),
        scratch_shapes=[pltpu.VMEM((tm + CONV_HALO, CONV_COLS), F32),
                        pltpu.VMEM((SUBLANES - 1, tm + CONV_HALO - SUBLANES, CONV_COLS), F32),
                        pltpu.VMEM((n_cb, tm, CONV_COLS), F32)],
        compiler_params=_params("arbitrary", "arbitrary"),
        name="conv_branch",
    )(y, y, dw_w, dw_b, ln_g, ln_b)


def _merge_kernel(o_ref, z_ref, wa_ref, wc_ref, bc_ref, ga_ref, gc_ref, out_ref):
    a = jnp.dot(o_ref[...], wa_ref[...], preferred_element_type=F32)
    c = jnp.dot(z_ref[...], wc_ref[...], preferred_element_type=F32) + bc_ref[...]
    out_ref[...] = (ga_ref[...].astype(F32) * a + gc_ref[...].astype(F32) * c).astype(out_ref.dtype)


def _merge(o, z, wa, wc, bc, gates, tm, tn):
    S, D = o.shape
    nb = D // tn
    return pl.pallas_call(
        _merge_kernel,
        grid=(S // tm, nb),
        in_specs=[pl.BlockSpec((tm, D), lambda i, j: (i, 0)),
                  pl.BlockSpec((tm, D), lambda i, j: (i, 0)),
                  pl.BlockSpec((D, tn), lambda i, j: (0, j)),
                  pl.BlockSpec((D, tn), lambda i, j: (0, j)),
                  pl.BlockSpec((1, tn), lambda i, j: (0, j)),
                  pl.BlockSpec((tm, tn), lambda i, j: (i, j)),
                  pl.BlockSpec((tm, tn), lambda i, j: (i, nb + j))],
        out_specs=pl.BlockSpec((tm, tn), lambda i, j: (i, j)),
        out_shape=jax.ShapeDtypeStruct((S, D), BF16),
        compiler_params=_params("arbitrary", "arbitrary"),
        name="merge",
    )(o, z, wa, wc, bc, gates, gates)


def _mixout_kernel(m_ref, w_ref, x_ref, g_ref, h_ref, f_ref):
    h = x_ref[...] + jnp.dot(m_ref[...], w_ref[...], preferred_element_type=F32)
    h_ref[...] = h
    f_ref[...] = _rms_rows(h, g_ref[...]).astype(f_ref.dtype)


def _mixout(mixed, w, x, g, tm):
    S, D = x.shape
    tile = pl.BlockSpec((tm, D), lambda i: (i, 0))
    return pl.pallas_call(
        _mixout_kernel,
        grid=(S // tm,),
        in_specs=[tile,
                  pl.BlockSpec((D, D), lambda i: (0, 0)),
                  tile,
                  pl.BlockSpec((1, D), lambda i: (0, 0))],
        out_specs=[tile, tile],
        out_shape=[jax.ShapeDtypeStruct((S, D), F32), jax.ShapeDtypeStruct((S, D), BF16)],
        compiler_params=_params("arbitrary"),
        name="mixout",
    )(mixed, w, x, g)


def _ffn_in_kernel(f_ref, wg_ref, wu_ref, a_ref):
    f = f_ref[...]
    g = jnp.dot(f, wg_ref[...].astype(BF16), preferred_element_type=F32)
    u = jnp.dot(f, wu_ref[...].astype(BF16), preferred_element_type=F32)
    a_ref[...] = (g * _sigmoid(g) * u).astype(a_ref.dtype)


def _ffn_in(f, w, tm, tn):
    S, D = f.shape
    d_ff = w.shape[1] // 2
    nb = d_ff // tn
    return pl.pallas_call(
        _ffn_in_kernel,
        grid=(S // tm, nb),
        in_specs=[pl.BlockSpec((tm, D), lambda i, j: (i, 0)),
                  pl.BlockSpec((D, tn), lambda i, j: (0, j)),
                  pl.BlockSpec((D, tn), lambda i, j: (0, nb + j))],
        out_specs=pl.BlockSpec((tm, tn), lambda i, j: (i, j)),
        out_shape=jax.ShapeDtypeStruct((S, d_ff), BF16),
        compiler_params=_params("arbitrary", "arbitrary"),
        name="ffn_in",
    )(f, w, w)


def _ffn_out_kernel(a_ref, w_ref, h_ref, g_ref, out_ref):
    k = pl.program_id(1)
    last = pl.num_programs(1) - 1

    def partial_product():
        return jnp.dot(a_ref[...], w_ref[...], preferred_element_type=F32)

    @pl.when(k == 0)
    def _():
        out_ref[...] = h_ref[...] + partial_product()

    @pl.when(jnp.logical_and(k > 0, k < last))
    def _():
        out_ref[...] += partial_product()

    @pl.when(k == last)
    def _():
        out_ref[...] = _rms_rows(out_ref[...] + partial_product(), g_ref[...])


def _ffn_out(a, w, h, g, tm, tk):
    S, D = h.shape
    d_ff = a.shape[1]
    assert d_ff // tk >= 2
    return pl.pallas_call(
        _ffn_out_kernel,
        grid=(S // tm, d_ff // tk),
        in_specs=[pl.BlockSpec((tm, tk), lambda i, k: (i, k)),
                  pl.BlockSpec((tk, D), lambda i, k: (k, 0)),
                  pl.BlockSpec((tm, D), lambda i, k: (i, 0)),
                  pl.BlockSpec((1, D), lambda i, k: (0, 0))],
        out_specs=pl.BlockSpec((tm, D), lambda i, k: (i, 0)),
        out_shape=jax.ShapeDtypeStruct((S, D), F32),
        compiler_params=_params("arbitrary", "arbitrary"),
        name="ffn_out",
    )(a, w, h, g)


def _tiles(S, d_ff):
    return dict(
        proj_rows=min(S, 1024), proj_cols=1024, glu_cols=512,
        attn=min(S, 512),
        conv_rows=min(S, 512),
        merge_rows=min(S, 1024), merge_cols=1024,
        mix_rows=min(S, 512),
        ffn_rows=min(S, 1024), ffn_cols=512,
        out_rows=min(S, 1024), out_k=d_ff // 4,
    )


def kernel(x, norm_mix_g, w_in, lambda_q1, lambda_k1, lambda_q2, lambda_k2, subln_g, w_attn_out, dw_w, dw_b,
           conv_ln_g, conv_ln_b, w_conv_out, b_conv_out, w_mix_out, norm_ffn_g, w_ffn_in, w_ffn_out, norm_final_g):
    B, S, D = x.shape
    depth = w_in.shape[0]
    assert B == 1 and depth == 1 and D == N_HEADS * HEAD_WIDTH
    d_ff = w_ffn_out.shape[1]
    t = _tiles(S, d_ff)
    row = lambda v: v.reshape(1, -1).astype(F32)

    h0 = x.reshape(S, D)
    w_in_f = w_in[0]

    q_scale = HEAD_DIM ** -0.5 * LOG2E
    col_scale = jnp.concatenate([jnp.full((1, D), q_scale, F32), jnp.ones((1, 2 * D), F32)], axis=1)
    pc = t["proj_cols"]
    gc = t["glu_cols"]
    w_glu = w_in_f[:, 3 * D:5 * D].astype(BF16)
    y, u = _norm_glu(h0, row(norm_mix_g[0]), w_glu, 0, D // gc, t["proj_rows"], gc)
    qkv = _proj_scaled(u, w_in_f, col_scale, 3 * D, t["proj_rows"], pc)
    gates = _proj_sigmoid(u, w_in_f, 5 * D // pc, 2 * D, t["proj_rows"], pc)

    o = _attention(qkv, row(lambda_q1[0]), row(lambda_k1[0]), row(lambda_q2[0]), row(lambda_k2[0]),
                   row(subln_g[0]), t["attn"])
    z = _conv_branch(y, dw_w[0].astype(F32), row(dw_b[0]), row(conv_ln_g[0]), row(conv_ln_b[0]), t["conv_rows"])

    mixed = _merge(o, z, w_attn_out[0].astype(BF16), w_conv_out[0].astype(BF16), row(b_conv_out[0]), gates,
                   t["merge_rows"], t["merge_cols"])
    h1, f = _mixout(mixed, w_mix_out[0].astype(BF16), h0, row(norm_ffn_g[0]), t["mix_rows"])

    a = _ffn_in(f, w_ffn_in[0], t["ffn_rows"], t["ffn_cols"])
    out = _ffn_out(a, w_ffn_out[0].astype(BF16), h1, row(norm_final_g), t["out_rows"], t["out_k"])
    return out.reshape(B, S, D)
```

```python
import functools
import math

import jax
import jax.numpy as jnp
import numpy as np
from jax import lax
from jax.experimental import pallas as pl
from jax.experimental.pallas import tpu as pltpu

F32 = jnp.float32
BF16 = jnp.bfloat16

N_HEADS = 8
HEAD_DIM = 128
HEAD_WIDTH = 2 * HEAD_DIM
CHUNK = 64
CONV_KERNEL = 31
CONV_HALO = 32
NORM_EPS = 1e-5
LAMBDA_INIT = 0.8 - 0.6 * math.exp(-0.3 * 0)
LOG2E = math.log2(math.e)
LANES = 128
VMEM_LIMIT = 60 * 1024 * 1024
NEG_BIG = -1e30


def _params(*sem):
    return pltpu.CompilerParams(dimension_semantics=sem, vmem_limit_bytes=VMEM_LIMIT)


def _rms_rows(x, g):
    return x * lax.rsqrt(jnp.mean(x * x, axis=-1, keepdims=True) + NORM_EPS) * g


def _sigmoid(x):
    return 1.0 / (1.0 + jnp.exp(-x))


NORM_ROWS = 256


def _norm_into(x_ref, g_ref, u_ref):
    @pl.when(pl.program_id(1) == 0)
    def _():
        def body(r, c):
            rows = pl.ds(pl.multiple_of(r * NORM_ROWS, NORM_ROWS), NORM_ROWS)
            u_ref[rows, :] = _rms_rows(x_ref[rows, :], g_ref[...]).astype(BF16)
            return c

        lax.fori_loop(0, x_ref.shape[0] // NORM_ROWS, body, 0)


def _norm_glu_kernel(x_ref, g_ref, wa_ref, wb_ref, y_ref, u_ref):
    _norm_into(x_ref, g_ref, u_ref)
    u = u_ref[...]
    gate = _sigmoid(jnp.dot(u, wb_ref[...], preferred_element_type=F32))
    a = jnp.dot(u, wa_ref[...], preferred_element_type=F32)
    y_ref[...] = (a * gate).astype(y_ref.dtype)


def _norm_glu(x, g, w, first_a, first_b, tm, tn):
    S, D = x.shape
    return pl.pallas_call(
        _norm_glu_kernel,
        grid=(S // tm, D // tn),
        in_specs=[pl.BlockSpec((tm, D), lambda i, j: (i, 0)),
                  pl.BlockSpec((1, D), lambda i, j: (0, 0)),
                  pl.BlockSpec((D, tn), lambda i, j: (0, first_a + j)),
                  pl.BlockSpec((D, tn), lambda i, j: (0, first_b + j))],
        out_specs=[pl.BlockSpec((tm, tn), lambda i, j: (i, j)),
                   pl.BlockSpec((tm, D), lambda i, j: (i, 0))],
        out_shape=[jax.ShapeDtypeStruct((S, D), BF16), jax.ShapeDtypeStruct((S, D), BF16)],
        compiler_params=_params("arbitrary", "arbitrary"),
        name="norm_glu",
    )(x, g, w, w)


def _proj_scaled_kernel(u_ref, w_ref, cs_ref, o_ref):
    r = jnp.dot(u_ref[...], w_ref[...].astype(BF16), preferred_element_type=F32)
    o_ref[...] = (r * cs_ref[...]).astype(o_ref.dtype)


def _proj_scaled(u, w, col_scale, n_out, tm, tn):
    S, D = u.shape
    return pl.pallas_call(
        _proj_scaled_kernel,
        grid=(S // tm, n_out // tn),
        in_specs=[pl.BlockSpec((tm, D), lambda i, j: (i, 0)),
                  pl.BlockSpec((D, tn), lambda i, j: (0, j)),
                  pl.BlockSpec((1, tn), lambda i, j: (0, j))],
        out_specs=pl.BlockSpec((tm, tn), lambda i, j: (i, j)),
        out_shape=jax.ShapeDtypeStruct((S, n_out), BF16),
        compiler_params=_params("arbitrary", "arbitrary"),
        name="proj_qkv",
    )(u, w, col_scale)


def _proj_sigmoid_kernel(u_ref, w_ref, o_ref):
    r = jnp.dot(u_ref[...], w_ref[...].astype(BF16), preferred_element_type=F32)
    o_ref[...] = _sigmoid(r).astype(o_ref.dtype)


def _proj_sigmoid(u, w, first_col_block, n_out, tm, tn):
    S, D = u.shape
    return pl.pallas_call(
        _proj_sigmoid_kernel,
        grid=(S // tm, n_out // tn),
        in_specs=[pl.BlockSpec((tm, D), lambda i, j: (i, 0)),
                  pl.BlockSpec((D, tn), lambda i, j: (0, first_col_block + j))],
        out_specs=pl.BlockSpec((tm, tn), lambda i, j: (i, j)),
        out_shape=jax.ShapeDtypeStruct((S, n_out), BF16),
        compiler_params=_params("arbitrary", "arbitrary"),
        name="proj_gates",
    )(u, w)


STAGE_UNROLL = 4


def _attn_kernel(slope_ref, q_ref, qn_ref, k_ref, v_ref, qaug_ref, kaug_ref, dx_ref, lq1_ref, lk1_ref, lq2_ref,
                 lk2_ref, g_ref, o_ref, m_ref, l_ref, acc_ref, sa_ref, xa_ref, sb_ref, xb_ref, sc_ref, xc_ref, *, t):
    w = 2 * t
    h = pl.program_id(0)
    i = pl.program_id(1)
    last_pair = i // 2
    tile_is_odd = i % 2 == 1
    slope = slope_ref[h]
    shift = slope * w
    reps = w // LANES

    q_extra = jnp.broadcast_to(qaug_ref[...], (t, HEAD_DIM)).astype(BF16)

    def augmented(ref):
        return tuple(jnp.concatenate([ref[:, j * HEAD_DIM:(j + 1) * HEAD_DIM], q_extra], axis=1) for j in range(2))

    q = augmented(q_ref)

    def pair_rows(pair):
        return pl.ds(pl.multiple_of(pair * w, w), w)

    def scores(pair, buf, q_maps=q):
        s_ref, x_ref = buf
        rows = pair_rows(pair)
        for j in range(2):
            k_j = jnp.concatenate([k_ref[rows, j * HEAD_DIM:(j + 1) * HEAD_DIM], kaug_ref[...]], axis=1)
            s = lax.dot_general(q_maps[j], k_j, (((1,), (1,)), ((), ())), preferred_element_type=F32)
            s_ref[j] = s
            lane_max = s[:, :LANES]
            for r in range(1, reps):
                lane_max = jnp.maximum(lane_max, s[:, r * LANES:(r + 1) * LANES])
            x_ref[j] = lane_max

    FULL, PAIR_DIAG, SINGLE_DIAG = range(3)

    def softmax_pv(pair, buf, kind, first=False):
        s_ref, x_ref = buf
        width = t if kind == SINGLE_DIAG else w
        reps = width // LANES
        ps, alphas = [], []
        for j in range(2):
            if kind == FULL:
                s = s_ref[j]
                m_cur = jnp.max(x_ref[j], axis=1, keepdims=True)
            else:
                bias = slope * dx_ref[...]
                if kind == PAIR_DIAG:
                    s = jnp.concatenate([s_ref[j, :, :t], s_ref[j, :, t:] + bias], axis=1)
                else:
                    s = s_ref[j, :, :t] + bias
                m_cur = jnp.max(s, axis=1, keepdims=True)
            if first:
                m_new = jnp.broadcast_to(m_cur, (t, LANES))
            else:
                m_prev = m_ref[j] - shift
                m_new = jnp.maximum(m_prev, m_cur)
                alpha = jnp.exp2(m_prev - m_new)
                alphas.append(jnp.concatenate([alpha] * (HEAD_WIDTH // LANES), axis=1))
            p = jnp.exp2(s - jnp.concatenate([m_new] * reps, axis=1))
            lane_sums = p[:, :LANES]
            for r in range(1, reps):
                lane_sums = lane_sums + p[:, r * LANES:(r + 1) * LANES]
            l_ref[j] = lane_sums if first else alpha * l_ref[j] + lane_sums
            m_ref[j] = m_new
            ps.append(p.astype(BF16))
        v_rows = pl.ds(pl.multiple_of(pair * w, w), width)
        pv = jnp.dot(jnp.concatenate(ps, axis=0), v_ref[v_rows, :], preferred_element_type=F32)
        acc_ref[...] = pv if first else acc_ref[...] * jnp.concatenate(alphas, axis=0) + pv

    buf_a = (sa_ref, xa_ref)
    buf_b = (sb_ref, xb_ref)
    buf_c = (sc_ref, xc_ref)

    def stage(pair, cur, nxt):
        scores(pair + 1, nxt)
        softmax_pv(pair, cur, FULL)

    def stages(first_pair, count):
        for d in range(0, count, 2):
            stage(first_pair + d, buf_a, buf_b)
            stage(first_pair + d + 1, buf_b, buf_a)

    def diagonal_and_next(buf, first=False):
        for parity, kind in ((True, PAIR_DIAG), (False, SINGLE_DIAG)):
            @pl.when(tile_is_odd == parity)
            def _(kind=kind):
                softmax_pv(last_pair, buf, kind, first=first)
                scores(0, buf_c, augmented(qn_ref))

    @pl.when(i == 0)
    def _():
        scores(0, buf_c)

    @pl.when(last_pair == 0)
    def _():
        diagonal_and_next(buf_c, first=True)

    @pl.when(last_pair > 0)
    def _():
        scores(1, buf_a)
        softmax_pv(0, buf_c, FULL, first=True)

    rest = jnp.maximum(last_pair - 1, 0)

    def unrolled(u, c):
        stages(1 + STAGE_UNROLL * u, STAGE_UNROLL)
        return c

    lax.fori_loop(0, rest // STAGE_UNROLL, unrolled, 0)
    piece = STAGE_UNROLL // 2
    while piece >= 2:
        @pl.when((rest & piece) != 0)
        def _(piece=piece):
            stages(1 + (rest & ~(2 * piece - 1)), piece)

        piece //= 2

    odd = rest % 2 == 1

    @pl.when(jnp.logical_and(last_pair > 0, odd))
    def _():
        stage(last_pair - 1, buf_a, buf_b)
        diagonal_and_next(buf_b)

    @pl.when(jnp.logical_and(last_pair > 0, jnp.logical_not(odd)))
    def _():
        diagonal_and_next(buf_a)

    lam = (jnp.exp(jnp.sum(lq1_ref[...] * lk1_ref[...], axis=1, keepdims=True))
           - jnp.exp(jnp.sum(lq2_ref[...] * lk2_ref[...], axis=1, keepdims=True)) + LAMBDA_INIT)
    o1 = acc_ref[:t, :] / jnp.sum(l_ref[0], axis=1, keepdims=True)
    o2 = acc_ref[t:, :] / jnp.sum(l_ref[1], axis=1, keepdims=True)
    o = o1 - lam * o2
    o_ref[...] = (_rms_rows(o, g_ref[...]) * (1.0 - LAMBDA_INIT)).astype(o_ref.dtype)


BIAS_SPLIT = 256
LOG2E_PIECES = 3


def _diag_extra(t):
    r = lax.broadcasted_iota(jnp.int32, (t, t), 0)
    c = lax.broadcasted_iota(jnp.int32, (t, t), 1)
    allowed = (c // CHUNK) <= (r // CHUNK)
    return jnp.where(allowed, (-2 * jnp.maximum(c - r, 0)).astype(F32), -jnp.inf)


def _bias_columns(w):
    pieces, rest = [], LOG2E
    for _ in range(LOG2E_PIECES):
        p = float(np.asarray(rest, np.float32).astype(BF16).astype(np.float32))
        pieces.append(p)
        rest -= p
    q_cols = jnp.zeros((1, HEAD_DIM), F32).at[0, :2 * LOG2E_PIECES].set(jnp.asarray(pieces * 2, F32))
    slopes = jnp.exp2(-(jnp.arange(N_HEADS, dtype=F32) + 1.0) * (8.0 / N_HEADS))
    c = jnp.arange(w, dtype=jnp.int32)
    lo = (c % BIAS_SPLIT).astype(F32)
    hi = (c - c % BIAS_SPLIT).astype(F32)
    k_cols = jnp.concatenate([jnp.tile(lo[:, None], (1, LOG2E_PIECES)), jnp.tile(hi[:, None], (1, LOG2E_PIECES)),
                              jnp.zeros((w, HEAD_DIM - 2 * LOG2E_PIECES), F32)], axis=1)
    k_cols = (slopes[:, None, None] * k_cols[None]).astype(BF16)
    return q_cols, k_cols, slopes * LOG2E


def _attention(qkv, lq1, lk1, lq2, lk2, subln_g, t):
    S = qkv.shape[0]
    n_tiles = S // t
    w = 2 * t
    assert n_tiles % 2 == 0
    q_cols, k_cols, slopes = _bias_columns(w)
    vec = pl.BlockSpec((1, HEAD_DIM), lambda h, i: (0, 0))
    return pl.pallas_call(
        functools.partial(_attn_kernel, t=t),
        grid=(N_HEADS, n_tiles),
        in_specs=[pl.BlockSpec(memory_space=pltpu.SMEM),
                  pl.BlockSpec((t, HEAD_WIDTH), lambda h, i: (i, h)),
                  pl.BlockSpec((t, HEAD_WIDTH), lambda h, i: (jnp.minimum(i + 1, n_tiles - 1), h)),
                  pl.BlockSpec((S, HEAD_WIDTH), lambda h, i: (0, N_HEADS + h)),
                  pl.BlockSpec((S, HEAD_WIDTH), lambda h, i: (0, 2 * N_HEADS + h)),
                  vec,
                  pl.BlockSpec((None, w, HEAD_DIM), lambda h, i: (h, 0, 0)),
                  pl.BlockSpec((t, t), lambda h, i: (0, 0), pipeline_mode=pl.Buffered(1)),
                  vec, vec, vec, vec,
                  pl.BlockSpec((1, HEAD_WIDTH), lambda h, i: (0, 0))],
        out_specs=pl.BlockSpec((t, HEAD_WIDTH), lambda h, i: (i, h)),
        out_shape=jax.ShapeDtypeStruct((S, N_HEADS * HEAD_WIDTH), BF16),
        scratch_shapes=[pltpu.VMEM((2, t, LANES), F32),
                        pltpu.VMEM((2, t, LANES), F32),
                        pltpu.VMEM((2 * t, HEAD_WIDTH), F32)] + 3 * [
                            pltpu.VMEM((2, t, w), F32),
                            pltpu.VMEM((2, t, LANES), F32)],
        compiler_params=_params("arbitrary", "arbitrary"),
        name="attention",
    )(slopes, qkv, qkv, qkv, qkv, q_cols, k_cols, _diag_extra(t), lq1, lk1, lq2, lk2, subln_g)


CONV_ROWS = 128
CONV_COLS = 256
SUBLANES = 8


def _conv_kernel(yp_ref, y_ref, w_ref, b_ref, lg_ref, lb_ref, z_ref, ext_ref, xs_ref, cv_ref):
    tm = y_ref.shape[0]
    cb = pl.program_id(1)
    n_cb = pl.num_programs(1)
    first = pl.program_id(0) == 0
    ext_ref[:CONV_HALO, :] = jnp.where(first, 0.0, yp_ref[...].astype(F32))
    ext_ref[CONV_HALO:, :] = y_ref[...].astype(F32)
    shifted_rows = xs_ref.shape[1]
    for r in range(1, SUBLANES):
        xs_ref[r - 1] = ext_ref[r:r + shifted_rows, :]
    lead = CONV_HALO - (CONV_KERNEL - 1)
    for r0 in range(0, tm, CONV_ROWS):
        acc = jnp.broadcast_to(b_ref[...], (CONV_ROWS, CONV_COLS))
        for j in range(CONV_KERNEL):
            r, a = (lead + j) % SUBLANES, (lead + j) // SUBLANES
            src = ext_ref if r == 0 else xs_ref.at[r - 1]
            lo = r0 + SUBLANES * a
            acc = acc + src[lo:lo + CONV_ROWS, :] * w_ref[j:j + 1, :]
        cv_ref[cb, r0:r0 + CONV_ROWS, :] = acc

    @pl.when(cb == n_cb - 1)
    def _():
        n_blocks = cv_ref.shape[0]
        n_ch = n_blocks * CONV_COLS

        def rows_body(rc, c):
            rows = pl.ds(pl.multiple_of(rc * CONV_ROWS, CONV_ROWS), CONV_ROWS)
            blocks = [cv_ref[k, rows, :] for k in range(n_blocks)]
            mu = sum(jnp.sum(b, axis=1, keepdims=True) for b in blocks) / n_ch
            var = sum(jnp.sum((b - mu) * (b - mu), axis=1, keepdims=True) for b in blocks) / n_ch
            inv = lax.rsqrt(var + NORM_EPS)
            for k, b in enumerate(blocks):
                cols = slice(k * CONV_COLS, (k + 1) * CONV_COLS)
                zn = (b - mu) * inv * lg_ref[:, cols] + lb_ref[:, cols]
                z_ref[rows, cols] = (zn * _sigmoid(zn)).astype(z_ref.dtype)
            return c

        lax.fori_loop(0, tm // CONV_ROWS, rows_body, 0)


def _conv_branch(y, dw_w, dw_b, ln_g, ln_b, tm):
    S, C = y.shape
    halo_blocks = tm // CONV_HALO
    n_cb = C // CONV_COLS
    row = pl.BlockSpec((1, C), lambda i, c: (0, 0))
    return pl.pallas_call(
        _conv_kernel,
        grid=(S // tm, n_cb),
        in_specs=[pl.BlockSpec((CONV_HALO, CONV_COLS), lambda i, c: (jnp.maximum(i * halo_blocks - 1, 0), c)),
                  pl.BlockSpec((tm, CONV_COLS), lambda i, c: (i, c)),
                  pl.BlockSpec((CONV_KERNEL, CONV_COLS), lambda i, c: (0, c)),
                  pl.BlockSpec((1, CONV_COLS), lambda i, c: (0, c)),
                  row, row],
        out_specs=pl.BlockSpec((tm, C), lambda i, c: (i, 0)),
        out_shape=jax.ShapeDtypeStruct((S, C), BF16),
        scratch_shapes=[pltpu.VMEM((tm + CONV_HALO, CONV_COLS), F32),
                        pltpu.VMEM((SUBLANES - 1, tm + CONV_HALO - SUBLANES, CONV_COLS), F32),
                        pltpu.VMEM((n_cb, tm, CONV_COLS), F32)],
        compiler_params=_params("arbitrary", "arbitrary"),
        name="conv_branch",
    )(y, y, dw_w, dw_b, ln_g, ln_b)


def _merge_kernel(o_ref, z_ref, wa_ref, wc_ref, bc_ref, ga_ref, gc_ref, out_ref):
    a = jnp.dot(o_ref[...], wa_ref[...], preferred_element_type=F32)
    c = jnp.dot(z_ref[...], wc_ref[...], preferred_element_type=F32) + bc_ref[...]
    out_ref[...] = (ga_ref[...].astype(F32) * a + gc_ref[...].astype(F32) * c).astype(out_ref.dtype)


def _merge(o, z, wa, wc, bc, gates, tm, tn):
    S, D = o.shape
    nb = D // tn
    return pl.pallas_call(
        _merge_kernel,
        grid=(S // tm, nb),
        in_specs=[pl.BlockSpec((tm, D), lambda i, j: (i, 0)),
                  pl.BlockSpec((tm, D), lambda i, j: (i, 0)),
                  pl.BlockSpec((D, tn), lambda i, j: (0, j)),
                  pl.BlockSpec((D, tn), lambda i, j: (0, j)),
                  pl.BlockSpec((1, tn), lambda i, j: (0, j)),
                  pl.BlockSpec((tm, tn), lambda i, j: (i, j)),
                  pl.BlockSpec((tm, tn), lambda i, j: (i, nb + j))],
        out_specs=pl.BlockSpec((tm, tn), lambda i, j: (i, j)),
        out_shape=jax.ShapeDtypeStruct((S, D), BF16),
        compiler_params=_params("arbitrary", "arbitrary"),
        name="merge",
    )(o, z, wa, wc, bc, gates, gates)


def _mixout_kernel(m_ref, w_ref, x_ref, g_ref, h_ref, f_ref):
    h = x_ref[...] + jnp.dot(m_ref[...], w_ref[...], preferred_element_type=F32)
    h_ref[...] = h
    f_ref[...] = _rms_rows(h, g_ref[...]).astype(f_ref.dtype)


def _mixout(mixed, w, x, g, tm):
    S, D = x.shape
    tile = pl.BlockSpec((tm, D), lambda i: (i, 0))
    return pl.pallas_call(
        _mixout_kernel,
        grid=(S // tm,),
        in_specs=[tile,
                  pl.BlockSpec((D, D), lambda i: (0, 0)),
                  tile,
                  pl.BlockSpec((1, D), lambda i: (0, 0))],
        out_specs=[tile, tile],
        out_shape=[jax.ShapeDtypeStruct((S, D), F32), jax.ShapeDtypeStruct((S, D), BF16)],
        compiler_params=_params("arbitrary"),
        name="mixout",
    )(mixed, w, x, g)


def _ffn_in_kernel(f_ref, wg_ref, wu_ref, a_ref):
    f = f_ref[...]
    g = jnp.dot(f, wg_ref[...].astype(BF16), preferred_element_type=F32)
    u = jnp.dot(f, wu_ref[...].astype(BF16), preferred_element_type=F32)
    a_ref[...] = (g * _sigmoid(g) * u).astype(a_ref.dtype)


def _ffn_in(f, w, tm, tn):
    S, D = f.shape
    d_ff = w.shape[1] // 2
    nb = d_ff // tn
    return pl.pallas_call(
        _ffn_in_kernel,
        grid=(S // tm, nb),
        in_specs=[pl.BlockSpec((tm, D), lambda i, j: (i, 0)),
                  pl.BlockSpec((D, tn), lambda i, j: (0, j)),
                  pl.BlockSpec((D, tn), lambda i, j: (0, nb + j))],
        out_specs=pl.BlockSpec((tm, tn), lambda i, j: (i, j)),
        out_shape=jax.ShapeDtypeStruct((S, d_ff), BF16),
        compiler_params=_params("arbitrary", "arbitrary"),
        name="ffn_in",
    )(f, w, w)


def _ffn_out_kernel(a_ref, w_ref, h_ref, g_ref, out_ref):
    k = pl.program_id(1)
    last = pl.num_programs(1) - 1

    def partial_product():
        return jnp.dot(a_ref[...], w_ref[...], preferred_element_type=F32)

    @pl.when(k == 0)
    def _():
        out_ref[...] = h_ref[...] + partial_product()

    @pl.when(jnp.logical_and(k > 0, k < last))
    def _():
        out_ref[...] += partial_product()

    @pl.when(k == last)
    def _():
        out_ref[...] = _rms_rows(out_ref[...] + partial_product(), g_ref[...])


def _ffn_out(a, w, h, g, tm, tk):
    S, D = h.shape
    d_ff = a.shape[1]
    assert d_ff // tk >= 2
    return pl.pallas_call(
        _ffn_out_kernel,
        grid=(S // tm, d_ff // tk),
        in_specs=[pl.BlockSpec((tm, tk), lambda i, k: (i, k)),
                  pl.BlockSpec((tk, D), lambda i, k: (k, 0)),
                  pl.BlockSpec((tm, D), lambda i, k: (i, 0)),
                  pl.BlockSpec((1, D), lambda i, k: (0, 0))],
        out_specs=pl.BlockSpec((tm, D), lambda i, k: (i, 0)),
        out_shape=jax.ShapeDtypeStruct((S, D), F32),
        compiler_params=_params("arbitrary", "arbitrary"),
        name="ffn_out",
    )(a, w, h, g)


def _tiles(S, d_ff):
    return dict(
        proj_rows=min(S, 1024), proj_cols=1024, glu_cols=512,
        attn=min(S, 512),
        conv_rows=min(S, 512),
        merge_rows=min(S, 1024), merge_cols=1024,
        mix_rows=min(S, 512),
        ffn_rows=min(S, 1024), ffn_cols=512,
        out_rows=min(S, 1024), out_k=d_ff // 4,
    )


def kernel(x, norm_mix_g, w_in, lambda_q1, lambda_k1, lambda_q2, lambda_k2, subln_g, w_attn_out, dw_w, dw_b,
           conv_ln_g, conv_ln_b, w_conv_out, b_conv_out, w_mix_out, norm_ffn_g, w_ffn_in, w_ffn_out, norm_final_g):
    B, S, D = x.shape
    depth = w_in.shape[0]
    assert B == 1 and depth == 1 and D == N_HEADS * HEAD_WIDTH
    d_ff = w_ffn_out.shape[1]
    t = _tiles(S, d_ff)
    row = lambda v: v.reshape(1, -1).astype(F32)

    h0 = x.reshape(S, D)
    w_in_f = w_in[0]

    q_scale = HEAD_DIM ** -0.5 * LOG2E
    col_scale = jnp.concatenate([jnp.full((1, D), q_scale, F32), jnp.ones((1, 2 * D), F32)], axis=1)
    pc = t["proj_cols"]
    gc = t["glu_cols"]
    w_glu = w_in_f[:, 3 * D:5 * D].astype(BF16)
    y, u = _norm_glu(h0, row(norm_mix_g[0]), w_glu, 0, D // gc, t["proj_rows"], gc)
    qkv = _proj_scaled(u, w_in_f, col_scale, 3 * D, t["proj_rows"], pc)
    gates = _proj_sigmoid(u, w_in_f, 5 * D // pc, 2 * D, t["proj_rows"], pc)

    o = _attention(qkv, row(lambda_q1[0]), row(lambda_k1[0]), row(lambda_q2[0]), row(lambda_k2[0]),
                   row(subln_g[0]), t["attn"])
    z = _conv_branch(y, dw_w[0].astype(F32), row(dw_b[0]), row(conv_ln_g[0]), row(conv_ln_b[0]), t["conv_rows"])

    mixed = _merge(o, z, w_attn_out[0].astype(BF16), w_conv_out[0].astype(BF16), row(b_conv_out[0]), gates,
                   t["merge_rows"], t["merge_cols"])
    h1, f = _mixout(mixed, w_mix_out[0].astype(BF16), h0, row(norm_ffn_g[0]), t["mix_rows"])

    a = _ffn_in(f, w_ffn_in[0], t["ffn_rows"], t["ffn_cols"])
    out = _ffn_out(a, w_ffn_out[0].astype(BF16), h1, row(norm_final_g), t["out_rows"], t["out_k"])
    return out.reshape(B, S, D)
```

```python
import functools
import math

import jax
import jax.numpy as jnp
import numpy as np
from jax import lax
from jax.experimental import pallas as pl
from jax.experimental.pallas import tpu as pltpu

F32 = jnp.float32
BF16 = jnp.bfloat16

N_HEADS = 8
HEAD_DIM = 128
HEAD_WIDTH = 2 * HEAD_DIM
CHUNK = 64
CONV_KERNEL = 31
CONV_HALO = 32
NORM_EPS = 1e-5
LAMBDA_INIT = 0.8 - 0.6 * math.exp(-0.3 * 0)
LOG2E = math.log2(math.e)
LANES = 128
VMEM_LIMIT = 60 * 1024 * 1024
NEG_BIG = -1e30


def _params(*sem):
    return pltpu.CompilerParams(dimension_semantics=sem, vmem_limit_bytes=VMEM_LIMIT)


def _rms_rows(x, g):
    return x * lax.rsqrt(jnp.mean(x * x, axis=-1, keepdims=True) + NORM_EPS) * g


def _sigmoid(x):
    return 1.0 / (1.0 + jnp.exp(-x))


NORM_ROWS = 256


def _norm_into(x_ref, g_ref, u_ref):
    @pl.when(pl.program_id(1) == 0)
    def _():
        def body(r, c):
            rows = pl.ds(pl.multiple_of(r * NORM_ROWS, NORM_ROWS), NORM_ROWS)
            u_ref[rows, :] = _rms_rows(x_ref[rows, :], g_ref[...]).astype(BF16)
            return c

        lax.fori_loop(0, x_ref.shape[0] // NORM_ROWS, body, 0)


def _norm_glu_kernel(x_ref, g_ref, wa_ref, wb_ref, y_ref, u_ref):
    _norm_into(x_ref, g_ref, u_ref)
    u = u_ref[...]
    gate = _sigmoid(jnp.dot(u, wb_ref[...], preferred_element_type=F32))
    a = jnp.dot(u, wa_ref[...], preferred_element_type=F32)
    y_ref[...] = (a * gate).astype(y_ref.dtype)


def _norm_glu(x, g, w, first_a, first_b, tm, tn):
    S, D = x.shape
    return pl.pallas_call(
        _norm_glu_kernel,
        grid=(S // tm, D // tn),
        in_specs=[pl.BlockSpec((tm, D), lambda i, j: (i, 0)),
                  pl.BlockSpec((1, D), lambda i, j: (0, 0)),
                  pl.BlockSpec((D, tn), lambda i, j: (0, first_a + j)),
                  pl.BlockSpec((D, tn), lambda i, j: (0, first_b + j))],
        out_specs=[pl.BlockSpec((tm, tn), lambda i, j: (i, j)),
                   pl.BlockSpec((tm, D), lambda i, j: (i, 0))],
        out_shape=[jax.ShapeDtypeStruct((S, D), BF16), jax.ShapeDtypeStruct((S, D), BF16)],
        compiler_params=_params("arbitrary", "arbitrary"),
        name="norm_glu",
    )(x, g, w, w)


def _proj_scaled_kernel(u_ref, w_ref, cs_ref, o_ref):
    r = jnp.dot(u_ref[...], w_ref[...].astype(BF16), preferred_element_type=F32)
    o_ref[...] = (r * cs_ref[...]).astype(o_ref.dtype)


def _proj_scaled(u, w, col_scale, n_out, tm, tn):
    S, D = u.shape
    return pl.pallas_call(
        _proj_scaled_kernel,
        grid=(S // tm, n_out // tn),
        in_specs=[pl.BlockSpec((tm, D), lambda i, j: (i, 0)),
                  pl.BlockSpec((D, tn), lambda i, j: (0, j)),
                  pl.BlockSpec((1, tn), lambda i, j: (0, j))],
        out_specs=pl.BlockSpec((tm, tn), lambda i, j: (i, j)),
        out_shape=jax.ShapeDtypeStruct((S, n_out), BF16),
        compiler_params=_params("arbitrary", "arbitrary"),
        name="proj_qkv",
    )(u, w, col_scale)


def _proj_sigmoid_kernel(u_ref, w_ref, o_ref):
    r = jnp.dot(u_ref[...], w_ref[...].astype(BF16), preferred_element_type=F32)
    o_ref[...] = _sigmoid(r).astype(o_ref.dtype)


def _proj_sigmoid(u, w, first_col_block, n_out, tm, tn):
    S, D = u.shape
    return pl.pallas_call(
        _proj_sigmoid_kernel,
        grid=(S // tm, n_out // tn),
        in_specs=[pl.BlockSpec((tm, D), lambda i, j: (i, 0)),
                  pl.BlockSpec((D, tn), lambda i, j: (0, first_col_block + j))],
        out_specs=pl.BlockSpec((tm, tn), lambda i, j: (i, j)),
        out_shape=jax.ShapeDtypeStruct((S, n_out), BF16),
        compiler_params=_params("arbitrary", "arbitrary"),
        name="proj_gates",
    )(u, w)


STAGE_UNROLL = 8


def _attn_kernel(slope_ref, q_ref, qn_ref, k_ref, v_ref, qaug_ref, kaug_ref, dx_ref, lq1_ref, lk1_ref, lq2_ref,
                 lk2_ref, g_ref, o_ref, m_ref, l_ref, acc_ref, sa_ref, xa_ref, sb_ref, xb_ref, sc_ref, xc_ref, *, t):
    w = 2 * t
    h = pl.program_id(0)
    i = pl.program_id(1)
    last_pair = i // 2
    tile_is_odd = i % 2 == 1
    slope = slope_ref[h]
    shift = slope * w
    reps = w // LANES

    q_extra = jnp.broadcast_to(qaug_ref[...], (t, HEAD_DIM)).astype(BF16)

    def augmented(ref):
        return tuple(jnp.concatenate([ref[:, j * HEAD_DIM:(j + 1) * HEAD_DIM], q_extra], axis=1) for j in range(2))

    q = augmented(q_ref)

    def pair_rows(pair):
        return pl.ds(pl.multiple_of(pair * w, w), w)

    def scores(pair, buf, q_maps=q):
        s_ref, x_ref = buf
        rows = pair_rows(pair)
        for j in range(2):
            k_j = jnp.concatenate([k_ref[rows, j * HEAD_DIM:(j + 1) * HEAD_DIM], kaug_ref[...]], axis=1)
            s = lax.dot_general(q_maps[j], k_j, (((1,), (1,)), ((), ())), preferred_element_type=F32)
            s_ref[j] = s
            lane_max = s[:, :LANES]
            for r in range(1, reps):
                lane_max = jnp.maximum(lane_max, s[:, r * LANES:(r + 1) * LANES])
            x_ref[j] = lane_max

    FULL, PAIR_DIAG, SINGLE_DIAG = range(3)

    def softmax_pv(pair, buf, kind, first=False):
        s_ref, x_ref = buf
        width = t if kind == SINGLE_DIAG else w
        reps = width // LANES
        ps, alphas = [], []
        for j in range(2):
            if kind == FULL:
                s = s_ref[j]
                m_cur = jnp.max(x_ref[j], axis=1, keepdims=True)
            else:
                bias = slope * dx_ref[...]
                if kind == PAIR_DIAG:
                    s = jnp.concatenate([s_ref[j, :, :t], s_ref[j, :, t:] + bias], axis=1)
                else:
                    s = s_ref[j, :, :t] + bias
                m_cur = jnp.max(s, axis=1, keepdims=True)
            if first:
                m_new = jnp.broadcast_to(m_cur, (t, LANES))
            else:
                m_prev = m_ref[j] - shift
                m_new = jnp.maximum(m_prev, m_cur)
                alpha = jnp.exp2(m_prev - m_new)
                alphas.append(jnp.concatenate([alpha] * (HEAD_WIDTH // LANES), axis=1))
            p = jnp.exp2(s - jnp.concatenate([m_new] * reps, axis=1))
            lane_sums = p[:, :LANES]
            for r in range(1, reps):
                lane_sums = lane_sums + p[:, r * LANES:(r + 1) * LANES]
            l_ref[j] = lane_sums if first else alpha * l_ref[j] + lane_sums
            m_ref[j] = m_new
            ps.append(p.astype(BF16))
        v_rows = pl.ds(pl.multiple_of(pair * w, w), width)
        pv = jnp.dot(jnp.concatenate(ps, axis=0), v_ref[v_rows, :], preferred_element_type=F32)
        acc_ref[...] = pv if first else acc_ref[...] * jnp.concatenate(alphas, axis=0) + pv

    buf_a = (sa_ref, xa_ref)
    buf_b = (sb_ref, xb_ref)
    buf_c = (sc_ref, xc_ref)

    def stage(pair, cur, nxt):
        scores(pair + 1, nxt)
        softmax_pv(pair, cur, FULL)

    def stages(first_pair, count):
        for d in range(0, count, 2):
            stage(first_pair + d, buf_a, buf_b)
            stage(first_pair + d + 1, buf_b, buf_a)

    def diagonal_and_next(buf, first=False):
        for parity, kind in ((True, PAIR_DIAG), (False, SINGLE_DIAG)):
            @pl.when(tile_is_odd == parity)
            def _(kind=kind):
                softmax_pv(last_pair, buf, kind, first=first)
                scores(0, buf_c, augmented(qn_ref))

    @pl.when(i == 0)
    def _():
        scores(0, buf_c)

    @pl.when(last_pair == 0)
    def _():
        diagonal_and_next(buf_c, first=True)

    @pl.when(last_pair > 0)
    def _():
        scores(1, buf_a)
        softmax_pv(0, buf_c, FULL, first=True)

    rest = jnp.maximum(last_pair - 1, 0)

    def unrolled(u, c):
        stages(1 + STAGE_UNROLL * u, STAGE_UNROLL)
        return c

    lax.fori_loop(0, rest // STAGE_UNROLL, unrolled, 0)
    piece = STAGE_UNROLL // 2
    while piece >= 2:
        @pl.when((rest & piece) != 0)
        def _(piece=piece):
            stages(1 + (rest & ~(2 * piece - 1)), piece)

        piece //= 2

    odd = rest % 2 == 1

    @pl.when(jnp.logical_and(last_pair > 0, odd))
    def _():
        stage(last_pair - 1, buf_a, buf_b)
        diagonal_and_next(buf_b)

    @pl.when(jnp.logical_and(last_pair > 0, jnp.logical_not(odd)))
    def _():
        diagonal_and_next(buf_a)

    lam = (jnp.exp(jnp.sum(lq1_ref[...] * lk1_ref[...], axis=1, keepdims=True))
           - jnp.exp(jnp.sum(lq2_ref[...] * lk2_ref[...], axis=1, keepdims=True)) + LAMBDA_INIT)
    o1 = acc_ref[:t, :] / jnp.sum(l_ref[0], axis=1, keepdims=True)
    o2 = acc_ref[t:, :] / jnp.sum(l_ref[1], axis=1, keepdims=True)
    o = o1 - lam * o2
    o_ref[...] = (_rms_rows(o, g_ref[...]) * (1.0 - LAMBDA_INIT)).astype(o_ref.dtype)


BIAS_SPLIT = 256
LOG2E_PIECES = 3


def _diag_extra(t):
    r = lax.broadcasted_iota(jnp.int32, (t, t), 0)
    c = lax.broadcasted_iota(jnp.int32, (t, t), 1)
    allowed = (c // CHUNK) <= (r // CHUNK)
    return jnp.where(allowed, (-2 * jnp.maximum(c - r, 0)).astype(F32), -jnp.inf)


def _bias_columns(w):
    pieces, rest = [], LOG2E
    for _ in range(LOG2E_PIECES):
        p = float(np.asarray(rest, np.float32).astype(BF16).astype(np.float32))
        pieces.append(p)
        rest -= p
    q_cols = jnp.zeros((1, HEAD_DIM), F32).at[0, :2 * LOG2E_PIECES].set(jnp.asarray(pieces * 2, F32))
    slopes = jnp.exp2(-(jnp.arange(N_HEADS, dtype=F32) + 1.0) * (8.0 / N_HEADS))
    c = jnp.arange(w, dtype=jnp.int32)
    lo = (c % BIAS_SPLIT).astype(F32)
    hi = (c - c % BIAS_SPLIT).astype(F32)
    k_cols = jnp.concatenate([jnp.tile(lo[:, None], (1, LOG2E_PIECES)), jnp.tile(hi[:, None], (1, LOG2E_PIECES)),
                              jnp.zeros((w, HEAD_DIM - 2 * LOG2E_PIECES), F32)], axis=1)
    k_cols = (slopes[:, None, None] * k_cols[None]).astype(BF16)
    return q_cols, k_cols, slopes * LOG2E


def _attention(qkv, lq1, lk1, lq2, lk2, subln_g, t):
    S = qkv.shape[0]
    n_tiles = S // t
    w = 2 * t
    assert n_tiles % 2 == 0
    q_cols, k_cols, slopes = _bias_columns(w)
    vec = pl.BlockSpec((1, HEAD_DIM), lambda h, i: (0, 0))
    return pl.pallas_call(
        functools.partial(_attn_kernel, t=t),
        grid=(N_HEADS, n_tiles),
        in_specs=[pl.BlockSpec(memory_space=pltpu.SMEM),
                  pl.BlockSpec((t, HEAD_WIDTH), lambda h, i: (i, h)),
                  pl.BlockSpec((t, HEAD_WIDTH), lambda h, i: (jnp.minimum(i + 1, n_tiles - 1), h)),
                  pl.BlockSpec((S, HEAD_WIDTH), lambda h, i: (0, N_HEADS + h)),
                  pl.BlockSpec((S, HEAD_WIDTH), lambda h, i: (0, 2 * N_HEADS + h)),
                  vec,
                  pl.BlockSpec((None, w, HEAD_DIM), lambda h, i: (h, 0, 0)),
                  pl.BlockSpec((t, t), lambda h, i: (0, 0), pipeline_mode=pl.Buffered(1)),
                  vec, vec, vec, vec,
                  pl.BlockSpec((1, HEAD_WIDTH), lambda h, i: (0, 0))],
        out_specs=pl.BlockSpec((t, HEAD_WIDTH), lambda h, i: (i, h)),
        out_shape=jax.ShapeDtypeStruct((S, N_HEADS * HEAD_WIDTH), BF16),
        scratch_shapes=[pltpu.VMEM((2, t, LANES), F32),
                        pltpu.VMEM((2, t, LANES), F32),
                        pltpu.VMEM((2 * t, HEAD_WIDTH), F32)] + 3 * [
                            pltpu.VMEM((2, t, w), F32),
                            pltpu.VMEM((2, t, LANES), F32)],
        compiler_params=_params("arbitrary", "arbitrary"),
        name="attention",
    )(slopes, qkv, qkv, qkv, qkv, q_cols, k_cols, _diag_extra(t), lq1, lk1, lq2, lk2, subln_g)


CONV_ROWS = 128
CONV_COLS = 256
SUBLANES = 8


def _conv_kernel(yp_ref, y_ref, w_ref, b_ref, lg_ref, lb_ref, z_ref, ext_ref, xs_ref, cv_ref):
    tm = y_ref.shape[0]
    cb = pl.program_id(1)
    n_cb = pl.num_programs(1)
    first = pl.program_id(0) == 0
    ext_ref[:CONV_HALO, :] = jnp.where(first, 0.0, yp_ref[...].astype(F32))
    ext_ref[CONV_HALO:, :] = y_ref[...].astype(F32)
    shifted_rows = xs_ref.shape[1]
    for r in range(1, SUBLANES):
        xs_ref[r - 1] = ext_ref[r:r + shifted_rows, :]
    lead = CONV_HALO - (CONV_KERNEL - 1)
    for r0 in range(0, tm, CONV_ROWS):
        acc = jnp.broadcast_to(b_ref[...], (CONV_ROWS, CONV_COLS))
        for j in range(CONV_KERNEL):
            r, a = (lead + j) % SUBLANES, (lead + j) // SUBLANES
            src = ext_ref if r == 0 else xs_ref.at[r - 1]
            lo = r0 + SUBLANES * a
            acc = acc + src[lo:lo + CONV_ROWS, :] * w_ref[j:j + 1, :]
        cv_ref[cb, r0:r0 + CONV_ROWS, :] = acc

    @pl.when(cb == n_cb - 1)
    def _():
        n_blocks = cv_ref.shape[0]
        n_ch = n_blocks * CONV_COLS

        def rows_body(rc, c):
            rows = pl.ds(pl.multiple_of(rc * CONV_ROWS, CONV_ROWS), CONV_ROWS)
            blocks = [cv_ref[k, rows, :] for k in range(n_blocks)]
            mu = sum(jnp.sum(b, axis=1, keepdims=True) for b in blocks) / n_ch
            var = sum(jnp.sum((b - mu) * (b - mu), axis=1, keepdims=True) for b in blocks) / n_ch
            inv = lax.rsqrt(var + NORM_EPS)
            for k, b in enumerate(blocks):
                cols = slice(k * CONV_COLS, (k + 1) * CONV_COLS)
                zn = (b - mu) * inv * lg_ref[:, cols] + lb_ref[:, cols]
                z_ref[rows, cols] = (zn * _sigmoid(zn)).astype(z_ref.dtype)
            return c

        lax.fori_loop(0, tm // CONV_ROWS, rows_body, 0)


def _conv_branch(y, dw_w, dw_b, ln_g, ln_b, tm):
    S, C = y.shape
    halo_blocks = tm // CONV_HALO
    n_cb = C // CONV_COLS
    row = pl.BlockSpec((1, C), lambda i, c: (0, 0))
    return pl.pallas_call(
        _conv_kernel,
        grid=(S // tm, n_cb),
        in_specs=[pl.BlockSpec((CONV_HALO, CONV_COLS), lambda i, c: (jnp.maximum(i * halo_blocks - 1, 0), c)),
                  pl.BlockSpec((tm, CONV_COLS), lambda i, c: (i, c)),
                  pl.BlockSpec((CONV_KERNEL, CONV_COLS), lambda i, c: (0, c)),
                  pl.BlockSpec((1, CONV_COLS), lambda i, c: (0, c)),
                  row, row],
        out_specs=pl.BlockSpec((tm, C), lambda i, c: (i, 0)),
        out_shape=jax.ShapeDtypeStruct((S, C), BF16),
        scratch_shapes=[pltpu.VMEM((tm + CONV_HALO, CONV_COLS), F32),
                        pltpu.VMEM((SUBLANES - 1, tm + CONV_HALO - SUBLANES, CONV_COLS), F32),
                        pltpu.VMEM((n_cb, tm, CONV_COLS), F32)],
        compiler_params=_params("arbitrary", "arbitrary"),
        name="conv_branch",
    )(y, y, dw_w, dw_b, ln_g, ln_b)


def _merge_kernel(o_ref, z_ref, wa_ref, wc_ref, bc_ref, ga_ref, gc_ref, out_ref):
    a = jnp.dot(o_ref[...], wa_ref[...], preferred_element_type=F32)
    c = jnp.dot(z_ref[...], wc_ref[...], preferred_element_type=F32) + bc_ref[...]
    out_ref[...] = (ga_ref[...].astype(F32) * a + gc_ref[...].astype(F32) * c).astype(out_ref.dtype)


def _merge(o, z, wa, wc, bc, gates, tm, tn):
    S, D = o.shape
    nb = D // tn
    return pl.pallas_call(
        _merge_kernel,
        grid=(S // tm, nb),
        in_specs=[pl.BlockSpec((tm, D), lambda i, j: (i, 0)),
                  pl.BlockSpec((tm, D), lambda i, j: (i, 0)),
                  pl.BlockSpec((D, tn), lambda i, j: (0, j)),
                  pl.BlockSpec((D, tn), lambda i, j: (0, j)),
                  pl.BlockSpec((1, tn), lambda i, j: (0, j)),
                  pl.BlockSpec((tm, tn), lambda i, j: (i, j)),
                  pl.BlockSpec((tm, tn), lambda i, j: (i, nb + j))],
        out_specs=pl.BlockSpec((tm, tn), lambda i, j: (i, j)),
        out_shape=jax.ShapeDtypeStruct((S, D), BF16),
        compiler_params=_params("arbitrary", "arbitrary"),
        name="merge",
    )(o, z, wa, wc, bc, gates, gates)


def _mixout_kernel(m_ref, w_ref, x_ref, g_ref, h_ref, f_ref):
    h = x_ref[...] + jnp.dot(m_ref[...], w_ref[...], preferred_element_type=F32)
    h_ref[...] = h
    f_ref[...] = _rms_rows(h, g_ref[...]).astype(f_ref.dtype)


def _mixout(mixed, w, x, g, tm):
    S, D = x.shape
    tile = pl.BlockSpec((tm, D), lambda i: (i, 0))
    return pl.pallas_call(
        _mixout_kernel,
        grid=(S // tm,),
        in_specs=[tile,
                  pl.BlockSpec((D, D), lambda i: (0, 0)),
                  tile,
                  pl.BlockSpec((1, D), lambda i: (0, 0))],
        out_specs=[tile, tile],
        out_shape=[jax.ShapeDtypeStruct((S, D), F32), jax.ShapeDtypeStruct((S, D), BF16)],
        compiler_params=_params("arbitrary"),
        name="mixout",
    )(mixed, w, x, g)


def _ffn_in_kernel(f_ref, wg_ref, wu_ref, a_ref):
    f = f_ref[...]
    g = jnp.dot(f, wg_ref[...].astype(BF16), preferred_element_type=F32)
    u = jnp.dot(f, wu_ref[...].astype(BF16), preferred_element_type=F32)
    a_ref[...] = (g * _sigmoid(g) * u).astype(a_ref.dtype)


def _ffn_in(f, w, tm, tn):
    S, D = f.shape
    d_ff = w.shape[1] // 2
    nb = d_ff // tn
    return pl.pallas_call(
        _ffn_in_kernel,
        grid=(S // tm, nb),
        in_specs=[pl.BlockSpec((tm, D), lambda i, j: (i, 0)),
                  pl.BlockSpec((D, tn), lambda i, j: (0, j)),
                  pl.BlockSpec((D, tn), lambda i, j: (0, nb + j))],
        out_specs=pl.BlockSpec((tm, tn), lambda i, j: (i, j)),
        out_shape=jax.ShapeDtypeStruct((S, d_ff), BF16),
        compiler_params=_params("arbitrary", "arbitrary"),
        name="ffn_in",
    )(f, w, w)


def _ffn_out_kernel(a_ref, w_ref, h_ref, g_ref, out_ref):
    k = pl.program_id(1)
    last = pl.num_programs(1) - 1

    def partial_product():
        return jnp.dot(a_ref[...], w_ref[...], preferred_element_type=F32)

    @pl.when(k == 0)
    def _():
        out_ref[...] = h_ref[...] + partial_product()

    @pl.when(jnp.logical_and(k > 0, k < last))
    def _():
        out_ref[...] += partial_product()

    @pl.when(k == last)
    def _():
        out_ref[...] = _rms_rows(out_ref[...] + partial_product(), g_ref[...])


def _ffn_out(a, w, h, g, tm, tk):
    S, D = h.shape
    d_ff = a.shape[1]
    assert d_ff // tk >= 2
    return pl.pallas_call(
        _ffn_out_kernel,
        grid=(S // tm, d_ff // tk),
        in_specs=[pl.BlockSpec((tm, tk), lambda i, k: (i, k)),
                  pl.BlockSpec((tk, D), lambda i, k: (k, 0)),
                  pl.BlockSpec((tm, D), lambda i, k: (i, 0)),
                  pl.BlockSpec((1, D), lambda i, k: (0, 0))],
        out_specs=pl.BlockSpec((tm, D), lambda i, k: (i, 0)),
        out_shape=jax.ShapeDtypeStruct((S, D), F32),
        compiler_params=_params("arbitrary", "arbitrary"),
        name="ffn_out",
    )(a, w, h, g)


def _tiles(S, d_ff):
    return dict(
        proj_rows=min(S, 1024), proj_cols=1024, glu_cols=512,
        attn=min(S, 512),
        conv_rows=min(S, 512),
        merge_rows=min(S, 1024), merge_cols=1024,
        mix_rows=min(S, 512),
        ffn_rows=min(S, 1024), ffn_cols=512,
        out_rows=min(S, 1024), out_k=d_ff // 4,
    )


def kernel(x, norm_mix_g, w_in, lambda_q1, lambda_k1, lambda_q2, lambda_k2, subln_g, w_attn_out, dw_w, dw_b,
           conv_ln_g, conv_ln_b, w_conv_out, b_conv_out, w_mix_out, norm_ffn_g, w_ffn_in, w_ffn_out, norm_final_g):
    B, S, D = x.shape
    depth = w_in.shape[0]
    assert B == 1 and depth == 1 and D == N_HEADS * HEAD_WIDTH
    d_ff = w_ffn_out.shape[1]
    t = _tiles(S, d_ff)
    row = lambda v: v.reshape(1, -1).astype(F32)

    h0 = x.reshape(S, D)
    w_in_f = w_in[0]

    q_scale = HEAD_DIM ** -0.5 * LOG2E
    col_scale = jnp.concatenate([jnp.full((1, D), q_scale, F32), jnp.ones((1, 2 * D), F32)], axis=1)
    pc = t["proj_cols"]
    gc = t["glu_cols"]
    w_glu = w_in_f[:, 3 * D:5 * D].astype(BF16)
    y, u = _norm_glu(h0, row(norm_mix_g[0]), w_glu, 0, D // gc, t["proj_rows"], gc)
    qkv = _proj_scaled(u, w_in_f, col_scale, 3 * D, t["proj_rows"], pc)
    gates = _proj_sigmoid(u, w_in_f, 5 * D // pc, 2 * D, t["proj_rows"], pc)

    o = _attention(qkv, row(lambda_q1[0]), row(lambda_k1[0]), row(lambda_q2[0]), row(lambda_k2[0]),
                   row(subln_g[0]), t["attn"])
    z = _conv_branch(y, dw_w[0].astype(F32), row(dw_b[0]), row(conv_ln_g[0]), row(conv_ln_b[0]), t["conv_rows"])

    mixed = _merge(o, z, w_attn_out[0].astype(BF16), w_conv_out[0].astype(BF16), row(b_conv_out[0]), gates,
                   t["merge_rows"], t["merge_cols"])
    h1, f = _mixout(mixed, w_mix_out[0].astype(BF16), h0, row(norm_ffn_g[0]), t["mix_rows"])

    a = _ffn_in(f, w_ffn_in[0], t["ffn_rows"], t["ffn_cols"])
    out = _ffn_out(a, w_ffn_out[0].astype(BF16), h1, row(norm_final_g), t["out_rows"], t["out_k"])
    return out.reshape(B, S, D)
```

```python
import functools
import math

import jax
import jax.numpy as jnp
import numpy as np
from jax import lax
from jax.experimental import pallas as pl
from jax.experimental.pallas import tpu as pltpu

F32 = jnp.float32
BF16 = jnp.bfloat16

N_HEADS = 8
HEAD_DIM = 128
HEAD_WIDTH = 2 * HEAD_DIM
CHUNK = 64
CONV_KERNEL = 31
CONV_HALO = 32
NORM_EPS = 1e-5
LAMBDA_INIT = 0.8 - 0.6 * math.exp(-0.3 * 0)
LOG2E = math.log2(math.e)
LANES = 128
VMEM_LIMIT = 60 * 1024 * 1024
NEG_BIG = -1e30


def _params(*sem):
    return pltpu.CompilerParams(dimension_semantics=sem, vmem_limit_bytes=VMEM_LIMIT)


def _rms_rows(x, g):
    return x * lax.rsqrt(jnp.mean(x * x, axis=-1, keepdims=True) + NORM_EPS) * g


def _sigmoid(x):
    return 1.0 / (1.0 + jnp.exp(-x))


NORM_ROWS = 256


def _norm_into(x_ref, g_ref, u_ref):
    @pl.when(pl.program_id(1) == 0)
    def _():
        def body(r, c):
            rows = pl.ds(pl.multiple_of(r * NORM_ROWS, NORM_ROWS), NORM_ROWS)
            u_ref[rows, :] = _rms_rows(x_ref[rows, :], g_ref[...]).astype(BF16)
            return c

        lax.fori_loop(0, x_ref.shape[0] // NORM_ROWS, body, 0)


def _norm_glu_kernel(x_ref, g_ref, wa_ref, wb_ref, y_ref, u_ref):
    _norm_into(x_ref, g_ref, u_ref)
    u = u_ref[...]
    gate = _sigmoid(jnp.dot(u, wb_ref[...], preferred_element_type=F32))
    a = jnp.dot(u, wa_ref[...], preferred_element_type=F32)
    y_ref[...] = (a * gate).astype(y_ref.dtype)


def _norm_glu(x, g, w, first_a, first_b, tm, tn):
    S, D = x.shape
    return pl.pallas_call(
        _norm_glu_kernel,
        grid=(S // tm, D // tn),
        in_specs=[pl.BlockSpec((tm, D), lambda i, j: (i, 0)),
                  pl.BlockSpec((1, D), lambda i, j: (0, 0)),
                  pl.BlockSpec((D, tn), lambda i, j: (0, first_a + j)),
                  pl.BlockSpec((D, tn), lambda i, j: (0, first_b + j))],
        out_specs=[pl.BlockSpec((tm, tn), lambda i, j: (i, j)),
                   pl.BlockSpec((tm, D), lambda i, j: (i, 0))],
        out_shape=[jax.ShapeDtypeStruct((S, D), BF16), jax.ShapeDtypeStruct((S, D), BF16)],
        compiler_params=_params("arbitrary", "arbitrary"),
        name="norm_glu",
    )(x, g, w, w)


def _proj_scaled_kernel(u_ref, w_ref, cs_ref, o_ref):
    r = jnp.dot(u_ref[...], w_ref[...].astype(BF16), preferred_element_type=F32)
    o_ref[...] = (r * cs_ref[...]).astype(o_ref.dtype)


def _proj_scaled(u, w, col_scale, n_out, tm, tn):
    S, D = u.shape
    return pl.pallas_call(
        _proj_scaled_kernel,
        grid=(S // tm, n_out // tn),
        in_specs=[pl.BlockSpec((tm, D), lambda i, j: (i, 0)),
                  pl.BlockSpec((D, tn), lambda i, j: (0, j)),
                  pl.BlockSpec((1, tn), lambda i, j: (0, j))],
        out_specs=pl.BlockSpec((tm, tn), lambda i, j: (i, j)),
        out_shape=jax.ShapeDtypeStruct((S, n_out), BF16),
        compiler_params=_params("arbitrary", "arbitrary"),
        name="proj_qkv",
    )(u, w, col_scale)


def _proj_sigmoid_kernel(u_ref, w_ref, o_ref):
    r = jnp.dot(u_ref[...], w_ref[...].astype(BF16), preferred_element_type=F32)
    o_ref[...] = _sigmoid(r).astype(o_ref.dtype)


def _proj_sigmoid(u, w, first_col_block, n_out, tm, tn):
    S, D = u.shape
    return pl.pallas_call(
        _proj_sigmoid_kernel,
        grid=(S // tm, n_out // tn),
        in_specs=[pl.BlockSpec((tm, D), lambda i, j: (i, 0)),
                  pl.BlockSpec((D, tn), lambda i, j: (0, first_col_block + j))],
        out_specs=pl.BlockSpec((tm, tn), lambda i, j: (i, j)),
        out_shape=jax.ShapeDtypeStruct((S, n_out), BF16),
        compiler_params=_params("arbitrary", "arbitrary"),
        name="proj_gates",
    )(u, w)


STAGE_UNROLL = 8


def _attn_kernel(slope_ref, q_ref, qn_ref, k_ref, v_ref, qaug_ref, kaug_ref, dx_ref, lq1_ref, lk1_ref, lq2_ref,
                 lk2_ref, g_ref, o_ref, m_ref, l_ref, acc_ref, sa_ref, xa_ref, sb_ref, xb_ref, sc_ref, xc_ref, *, t):
    w = 2 * t
    h = pl.program_id(0)
    i = pl.program_id(1)
    last_pair = i // 2
    tile_is_odd = i % 2 == 1
    slope = slope_ref[h]
    shift = slope * w
    reps = w // LANES

    q_extra = jnp.broadcast_to(qaug_ref[...], (t, HEAD_DIM)).astype(BF16)

    def augmented(ref):
        return tuple(jnp.concatenate([ref[:, j * HEAD_DIM:(j + 1) * HEAD_DIM], q_extra], axis=1) for j in range(2))

    q = augmented(q_ref)

    def pair_rows(pair):
        return pl.ds(pl.multiple_of(pair * w, w), w)

    def scores(pair, buf, q_maps=q):
        s_ref, x_ref = buf
        rows = pair_rows(pair)
        for j in range(2):
            k_j = jnp.concatenate([k_ref[j * HEAD_DIM:(j + 1) * HEAD_DIM, rows], kaug_ref[...]], axis=0)
            s = jnp.dot(q_maps[j], k_j, preferred_element_type=F32)
            s_ref[j] = s
            lane_max = s[:, :LANES]
            for r in range(1, reps):
                lane_max = jnp.maximum(lane_max, s[:, r * LANES:(r + 1) * LANES])
            x_ref[j] = lane_max

    FULL, PAIR_DIAG, SINGLE_DIAG = range(3)

    def softmax_pv(pair, buf, kind, first=False):
        s_ref, x_ref = buf
        width = t if kind == SINGLE_DIAG else w
        reps = width // LANES
        ps, alphas = [], []
        for j in range(2):
            if kind == FULL:
                s = s_ref[j]
                m_cur = jnp.max(x_ref[j], axis=1, keepdims=True)
            else:
                bias = slope * dx_ref[...]
                if kind == PAIR_DIAG:
                    s = jnp.concatenate([s_ref[j, :, :t], s_ref[j, :, t:] + bias], axis=1)
                else:
                    s = s_ref[j, :, :t] + bias
                m_cur = jnp.max(s, axis=1, keepdims=True)
            if first:
                m_new = jnp.broadcast_to(m_cur, (t, LANES))
            else:
                m_prev = m_ref[j] - shift
                m_new = jnp.maximum(m_prev, m_cur)
                alpha = jnp.exp2(m_prev - m_new)
                alphas.append(jnp.concatenate([alpha] * (HEAD_WIDTH // LANES), axis=1))
            p = jnp.exp2(s - jnp.concatenate([m_new] * reps, axis=1))
            lane_sums = p[:, :LANES]
            for r in range(1, reps):
                lane_sums = lane_sums + p[:, r * LANES:(r + 1) * LANES]
            l_ref[j] = lane_sums if first else alpha * l_ref[j] + lane_sums
            m_ref[j] = m_new
            ps.append(p.astype(BF16))
        v_rows = pl.ds(pl.multiple_of(pair * w, w), width)
        pv = jnp.dot(jnp.concatenate(ps, axis=0), v_ref[v_rows, :], preferred_element_type=F32)
        acc_ref[...] = pv if first else acc_ref[...] * jnp.concatenate(alphas, axis=0) + pv

    buf_a = (sa_ref, xa_ref)
    buf_b = (sb_ref, xb_ref)
    buf_c = (sc_ref, xc_ref)

    def stage(pair, cur, nxt):
        scores(pair + 1, nxt)
        softmax_pv(pair, cur, FULL)

    def stages(first_pair, count):
        for d in range(0, count, 2):
            stage(first_pair + d, buf_a, buf_b)
            stage(first_pair + d + 1, buf_b, buf_a)

    def diagonal_and_next(buf, first=False):
        for parity, kind in ((True, PAIR_DIAG), (False, SINGLE_DIAG)):
            @pl.when(tile_is_odd == parity)
            def _(kind=kind):
                softmax_pv(last_pair, buf, kind, first=first)
                scores(0, buf_c, augmented(qn_ref))

    @pl.when(i == 0)
    def _():
        scores(0, buf_c)

    @pl.when(last_pair == 0)
    def _():
        diagonal_and_next(buf_c, first=True)

    @pl.when(last_pair > 0)
    def _():
        scores(1, buf_a)
        softmax_pv(0, buf_c, FULL, first=True)

    rest = jnp.maximum(last_pair - 1, 0)

    def unrolled(u, c):
        stages(1 + STAGE_UNROLL * u, STAGE_UNROLL)
        return c

    lax.fori_loop(0, rest // STAGE_UNROLL, unrolled, 0)
    piece = STAGE_UNROLL // 2
    while piece >= 2:
        @pl.when((rest & piece) != 0)
        def _(piece=piece):
            stages(1 + (rest & ~(2 * piece - 1)), piece)

        piece //= 2

    odd = rest % 2 == 1

    @pl.when(jnp.logical_and(last_pair > 0, odd))
    def _():
        stage(last_pair - 1, buf_a, buf_b)
        diagonal_and_next(buf_b)

    @pl.when(jnp.logical_and(last_pair > 0, jnp.logical_not(odd)))
    def _():
        diagonal_and_next(buf_a)

    lam = (jnp.exp(jnp.sum(lq1_ref[...] * lk1_ref[...], axis=1, keepdims=True))
           - jnp.exp(jnp.sum(lq2_ref[...] * lk2_ref[...], axis=1, keepdims=True)) + LAMBDA_INIT)
    o1 = acc_ref[:t, :] / jnp.sum(l_ref[0], axis=1, keepdims=True)
    o2 = acc_ref[t:, :] / jnp.sum(l_ref[1], axis=1, keepdims=True)
    o = o1 - lam * o2
    o_ref[...] = (_rms_rows(o, g_ref[...]) * (1.0 - LAMBDA_INIT)).astype(o_ref.dtype)


BIAS_SPLIT = 256
LOG2E_PIECES = 3


def _diag_extra(t):
    r = lax.broadcasted_iota(jnp.int32, (t, t), 0)
    c = lax.broadcasted_iota(jnp.int32, (t, t), 1)
    allowed = (c // CHUNK) <= (r // CHUNK)
    return jnp.where(allowed, (-2 * jnp.maximum(c - r, 0)).astype(F32), -jnp.inf)


def _bias_columns(w):
    pieces, rest = [], LOG2E
    for _ in range(LOG2E_PIECES):
        p = float(np.asarray(rest, np.float32).astype(BF16).astype(np.float32))
        pieces.append(p)
        rest -= p
    q_cols = jnp.zeros((1, HEAD_DIM), F32).at[0, :2 * LOG2E_PIECES].set(jnp.asarray(pieces * 2, F32))
    slopes = jnp.exp2(-(jnp.arange(N_HEADS, dtype=F32) + 1.0) * (8.0 / N_HEADS))
    c = jnp.arange(w, dtype=jnp.int32)
    lo = (c % BIAS_SPLIT).astype(F32)
    hi = (c - c % BIAS_SPLIT).astype(F32)
    k_cols = jnp.concatenate([jnp.tile(lo[:, None], (1, LOG2E_PIECES)), jnp.tile(hi[:, None], (1, LOG2E_PIECES)),
                              jnp.zeros((w, HEAD_DIM - 2 * LOG2E_PIECES), F32)], axis=1)
    k_cols = (slopes[:, None, None] * k_cols[None]).astype(BF16)
    return q_cols, k_cols, slopes * LOG2E


def _attention(qkv, lq1, lk1, lq2, lk2, subln_g, t):
    S = qkv.shape[0]
    n_tiles = S // t
    w = 2 * t
    assert n_tiles % 2 == 0
    q_cols, k_cols, slopes = _bias_columns(w)
    width = N_HEADS * HEAD_WIDTH
    k_t = qkv[:, width:2 * width].T
    vec = pl.BlockSpec((1, HEAD_DIM), lambda h, i: (0, 0))
    return pl.pallas_call(
        functools.partial(_attn_kernel, t=t),
        grid=(N_HEADS, n_tiles),
        in_specs=[pl.BlockSpec(memory_space=pltpu.SMEM),
                  pl.BlockSpec((t, HEAD_WIDTH), lambda h, i: (i, h)),
                  pl.BlockSpec((t, HEAD_WIDTH), lambda h, i: (jnp.minimum(i + 1, n_tiles - 1), h)),
                  pl.BlockSpec((HEAD_WIDTH, S), lambda h, i: (h, 0)),
                  pl.BlockSpec((S, HEAD_WIDTH), lambda h, i: (0, 2 * N_HEADS + h)),
                  vec,
                  pl.BlockSpec((None, HEAD_DIM, w), lambda h, i: (h, 0, 0)),
                  pl.BlockSpec((t, t), lambda h, i: (0, 0), pipeline_mode=pl.Buffered(1)),
                  vec, vec, vec, vec,
                  pl.BlockSpec((1, HEAD_WIDTH), lambda h, i: (0, 0))],
        out_specs=pl.BlockSpec((t, HEAD_WIDTH), lambda h, i: (i, h)),
        out_shape=jax.ShapeDtypeStruct((S, N_HEADS * HEAD_WIDTH), BF16),
        scratch_shapes=[pltpu.VMEM((2, t, LANES), F32),
                        pltpu.VMEM((2, t, LANES), F32),
                        pltpu.VMEM((2 * t, HEAD_WIDTH), F32)] + 3 * [
                            pltpu.VMEM((2, t, w), F32),
                            pltpu.VMEM((2, t, LANES), F32)],
        compiler_params=_params("arbitrary", "arbitrary"),
        name="attention",
    )(slopes, qkv, qkv, k_t, qkv, q_cols, jnp.swapaxes(k_cols, 1, 2), _diag_extra(t), lq1, lk1, lq2, lk2, subln_g)


CONV_ROWS = 128
CONV_COLS = 256
SUBLANES = 8


def _conv_kernel(yp_ref, y_ref, w_ref, b_ref, lg_ref, lb_ref, z_ref, ext_ref, xs_ref, cv_ref):
    tm = y_ref.shape[0]
    cb = pl.program_id(1)
    n_cb = pl.num_programs(1)
    first = pl.program_id(0) == 0
    ext_ref[:CONV_HALO, :] = jnp.where(first, 0.0, yp_ref[...].astype(F32))
    ext_ref[CONV_HALO:, :] = y_ref[...].astype(F32)
    shifted_rows = xs_ref.shape[1]
    for r in range(1, SUBLANES):
        xs_ref[r - 1] = ext_ref[r:r + shifted_rows, :]
    lead = CONV_HALO - (CONV_KERNEL - 1)
    for r0 in range(0, tm, CONV_ROWS):
        acc = jnp.broadcast_to(b_ref[...], (CONV_ROWS, CONV_COLS))
        for j in range(CONV_KERNEL):
            r, a = (lead + j) % SUBLANES, (lead + j) // SUBLANES
            src = ext_ref if r == 0 else xs_ref.at[r - 1]
            lo = r0 + SUBLANES * a
            acc = acc + src[lo:lo + CONV_ROWS, :] * w_ref[j:j + 1, :]
        cv_ref[cb, r0:r0 + CONV_ROWS, :] = acc

    @pl.when(cb == n_cb - 1)
    def _():
        n_blocks = cv_ref.shape[0]
        n_ch = n_blocks * CONV_COLS

        def rows_body(rc, c):
            rows = pl.ds(pl.multiple_of(rc * CONV_ROWS, CONV_ROWS), CONV_ROWS)
            blocks = [cv_ref[k, rows, :] for k in range(n_blocks)]
            mu = sum(jnp.sum(b, axis=1, keepdims=True) for b in blocks) / n_ch
            var = sum(jnp.sum((b - mu) * (b - mu), axis=1, keepdims=True) for b in blocks) / n_ch
            inv = lax.rsqrt(var + NORM_EPS)
            for k, b in enumerate(blocks):
                cols = slice(k * CONV_COLS, (k + 1) * CONV_COLS)
                zn = (b - mu) * inv * lg_ref[:, cols] + lb_ref[:, cols]
                z_ref[rows, cols] = (zn * _sigmoid(zn)).astype(z_ref.dtype)
            return c

        lax.fori_loop(0, tm // CONV_ROWS, rows_body, 0)


def _conv_branch(y, dw_w, dw_b, ln_g, ln_b, tm):
    S, C = y.shape
    halo_blocks = tm // CONV_HALO
    n_cb = C // CONV_COLS
    row = pl.BlockSpec((1, C), lambda i, c: (0, 0))
    return pl.pallas_call(
        _conv_kernel,
        grid=(S // tm, n_cb),
        in_specs=[pl.BlockSpec((CONV_HALO, CONV_COLS), lambda i, c: (jnp.maximum(i * halo_blocks - 1, 0), c)),
                  pl.BlockSpec((tm, CONV_COLS), lambda i, c: (i, c)),
                  pl.BlockSpec((CONV_KERNEL, CONV_COLS), lambda i, c: (0, c)),
                  pl.BlockSpec((1, CONV_COLS), lambda i, c: (0, c)),
                  row, row],
        out_specs=pl.BlockSpec((tm, C), lambda i, c: (i, 0)),
        out_shape=jax.ShapeDtypeStruct((S, C), BF16),
        scratch_shapes=[pltpu.VMEM((tm + CONV_HALO, CONV_COLS), F32),
                        pltpu.VMEM((SUBLANES - 1, tm + CONV_HALO - SUBLANES, CONV_COLS), F32),
                        pltpu.VMEM((n_cb, tm, CONV_COLS), F32)],
        compiler_params=_params("arbitrary", "arbitrary"),
        name="conv_branch",
    )(y, y, dw_w, dw_b, ln_g, ln_b)


def _merge_kernel(o_ref, z_ref, wa_ref, wc_ref, bc_ref, ga_ref, gc_ref, out_ref):
    a = jnp.dot(o_ref[...], wa_ref[...], preferred_element_type=F32)
    c = jnp.dot(z_ref[...], wc_ref[...], preferred_element_type=F32) + bc_ref[...]
    out_ref[...] = (ga_ref[...].astype(F32) * a + gc_ref[...].astype(F32) * c).astype(out_ref.dtype)


def _merge(o, z, wa, wc, bc, gates, tm, tn):
    S, D = o.shape
    nb = D // tn
    return pl.pallas_call(
        _merge_kernel,
        grid=(S // tm, nb),
        in_specs=[pl.BlockSpec((tm, D), lambda i, j: (i, 0)),
                  pl.BlockSpec((tm, D), lambda i, j: (i, 0)),
                  pl.BlockSpec((D, tn), lambda i, j: (0, j)),
                  pl.BlockSpec((D, tn), lambda i, j: (0, j)),
                  pl.BlockSpec((1, tn), lambda i, j: (0, j)),
                  pl.BlockSpec((tm, tn), lambda i, j: (i, j)),
                  pl.BlockSpec((tm, tn), lambda i, j: (i, nb + j))],
        out_specs=pl.BlockSpec((tm, tn), lambda i, j: (i, j)),
        out_shape=jax.ShapeDtypeStruct((S, D), BF16),
        compiler_params=_params("arbitrary", "arbitrary"),
        name="merge",
    )(o, z, wa, wc, bc, gates, gates)


def _mixout_kernel(m_ref, w_ref, x_ref, g_ref, h_ref, f_ref):
    h = x_ref[...] + jnp.dot(m_ref[...], w_ref[...], preferred_element_type=F32)
    h_ref[...] = h
    f_ref[...] = _rms_rows(h, g_ref[...]).astype(f_ref.dtype)


def _mixout(mixed, w, x, g, tm):
    S, D = x.shape
    tile = pl.BlockSpec((tm, D), lambda i: (i, 0))
    return pl.pallas_call(
        _mixout_kernel,
        grid=(S // tm,),
        in_specs=[tile,
                  pl.BlockSpec((D, D), lambda i: (0, 0)),
                  tile,
                  pl.BlockSpec((1, D), lambda i: (0, 0))],
        out_specs=[tile, tile],
        out_shape=[jax.ShapeDtypeStruct((S, D), F32), jax.ShapeDtypeStruct((S, D), BF16)],
        compiler_params=_params("arbitrary"),
        name="mixout",
    )(mixed, w, x, g)


def _ffn_in_kernel(f_ref, wg_ref, wu_ref, a_ref):
    f = f_ref[...]
    g = jnp.dot(f, wg_ref[...].astype(BF16), preferred_element_type=F32)
    u = jnp.dot(f, wu_ref[...].astype(BF16), preferred_element_type=F32)
    a_ref[...] = (g * _sigmoid(g) * u).astype(a_ref.dtype)


def _ffn_in(f, w, tm, tn):
    S, D = f.shape
    d_ff = w.shape[1] // 2
    nb = d_ff // tn
    return pl.pallas_call(
        _ffn_in_kernel,
        grid=(S // tm, nb),
        in_specs=[pl.BlockSpec((tm, D), lambda i, j: (i, 0)),
                  pl.BlockSpec((D, tn), lambda i, j: (0, j)),
                  pl.BlockSpec((D, tn), lambda i, j: (0, nb + j))],
        out_specs=pl.BlockSpec((tm, tn), lambda i, j: (i, j)),
        out_shape=jax.ShapeDtypeStruct((S, d_ff), BF16),
        compiler_params=_params("arbitrary", "arbitrary"),
        name="ffn_in",
    )(f, w, w)


def _ffn_out_kernel(a_ref, w_ref, h_ref, g_ref, out_ref):
    k = pl.program_id(1)
    last = pl.num_programs(1) - 1

    def partial_product():
        return jnp.dot(a_ref[...], w_ref[...], preferred_element_type=F32)

    @pl.when(k == 0)
    def _():
        out_ref[...] = h_ref[...] + partial_product()

    @pl.when(jnp.logical_and(k > 0, k < last))
    def _():
        out_ref[...] += partial_product()

    @pl.when(k == last)
    def _():
        out_ref[...] = _rms_rows(out_ref[...] + partial_product(), g_ref[...])


def _ffn_out(a, w, h, g, tm, tk):
    S, D = h.shape
    d_ff = a.shape[1]
    assert d_ff // tk >= 2
    return pl.pallas_call(
        _ffn_out_kernel,
        grid=(S // tm, d_ff // tk),
        in_specs=[pl.BlockSpec((tm, tk), lambda i, k: (i, k)),
                  pl.BlockSpec((tk, D), lambda i, k: (k, 0)),
                  pl.BlockSpec((tm, D), lambda i, k: (i, 0)),
                  pl.BlockSpec((1, D), lambda i, k: (0, 0))],
        out_specs=pl.BlockSpec((tm, D), lambda i, k: (i, 0)),
        out_shape=jax.ShapeDtypeStruct((S, D), F32),
        compiler_params=_params("arbitrary", "arbitrary"),
        name="ffn_out",
    )(a, w, h, g)


def _tiles(S, d_ff):
    return dict(
        proj_rows=min(S, 1024), proj_cols=1024, glu_cols=512,
        attn=min(S, 512),
        conv_rows=min(S, 512),
        merge_rows=min(S, 1024), merge_cols=1024,
        mix_rows=min(S, 512),
        ffn_rows=min(S, 1024), ffn_cols=512,
        out_rows=min(S, 1024), out_k=d_ff // 4,
    )


def kernel(x, norm_mix_g, w_in, lambda_q1, lambda_k1, lambda_q2, lambda_k2, subln_g, w_attn_out, dw_w, dw_b,
           conv_ln_g, conv_ln_b, w_conv_out, b_conv_out, w_mix_out, norm_ffn_g, w_ffn_in, w_ffn_out, norm_final_g):
    B, S, D = x.shape
    depth = w_in.shape[0]
    assert B == 1 and depth == 1 and D == N_HEADS * HEAD_WIDTH
    d_ff = w_ffn_out.shape[1]
    t = _tiles(S, d_ff)
    row = lambda v: v.reshape(1, -1).astype(F32)

    h0 = x.reshape(S, D)
    w_in_f = w_in[0]

    q_scale = HEAD_DIM ** -0.5 * LOG2E
    col_scale = jnp.concatenate([jnp.full((1, D), q_scale, F32), jnp.ones((1, 2 * D), F32)], axis=1)
    pc = t["proj_cols"]
    gc = t["glu_cols"]
    w_glu = w_in_f[:, 3 * D:5 * D].astype(BF16)
    y, u = _norm_glu(h0, row(norm_mix_g[0]), w_glu, 0, D // gc, t["proj_rows"], gc)
    qkv = _proj_scaled(u, w_in_f, col_scale, 3 * D, t["proj_rows"], pc)
    gates = _proj_sigmoid(u, w_in_f, 5 * D // pc, 2 * D, t["proj_rows"], pc)

    o = _attention(qkv, row(lambda_q1[0]), row(lambda_k1[0]), row(lambda_q2[0]), row(lambda_k2[0]),
                   row(subln_g[0]), t["attn"])
    z = _conv_branch(y, dw_w[0].astype(F32), row(dw_b[0]), row(conv_ln_g[0]), row(conv_ln_b[0]), t["conv_rows"])

    mixed = _merge(o, z, w_attn_out[0].astype(BF16), w_conv_out[0].astype(BF16), row(b_conv_out[0]), gates,
                   t["merge_rows"], t["merge_cols"])
    h1, f = _mixout(mixed, w_mix_out[0].astype(BF16), h0, row(norm_ffn_g[0]), t["mix_rows"])

    a = _ffn_in(f, w_ffn_in[0], t["ffn_rows"], t["ffn_cols"])
    out = _ffn_out(a, w_ffn_out[0].astype(BF16), h1, row(norm_final_g), t["out_rows"], t["out_k"])
    return out.reshape(B, S, D)
```

```python
import functools
import math

import jax
import jax.numpy as jnp
import numpy as np
from jax import lax
from jax.experimental import pallas as pl
from jax.experimental.pallas import tpu as pltpu

F32 = jnp.float32
BF16 = jnp.bfloat16

N_HEADS = 8
HEAD_DIM = 128
HEAD_WIDTH = 2 * HEAD_DIM
CHUNK = 64
CONV_KERNEL = 31
CONV_HALO = 32
NORM_EPS = 1e-5
LAMBDA_INIT = 0.8 - 0.6 * math.exp(-0.3 * 0)
LOG2E = math.log2(math.e)
LANES = 128
VMEM_LIMIT = 60 * 1024 * 1024
NEG_BIG = -1e30


def _params(*sem):
    return pltpu.CompilerParams(dimension_semantics=sem, vmem_limit_bytes=VMEM_LIMIT)


def _rms_rows(x, g):
    return x * lax.rsqrt(jnp.mean(x * x, axis=-1, keepdims=True) + NORM_EPS) * g


def _sigmoid(x):
    return 1.0 / (1.0 + jnp.exp(-x))


NORM_ROWS = 256


def _norm_into(x_ref, g_ref, u_ref):
    @pl.when(pl.program_id(1) == 0)
    def _():
        def body(r, c):
            rows = pl.ds(pl.multiple_of(r * NORM_ROWS, NORM_ROWS), NORM_ROWS)
            u_ref[rows, :] = _rms_rows(x_ref[rows, :], g_ref[...]).astype(BF16)
            return c

        lax.fori_loop(0, x_ref.shape[0] // NORM_ROWS, body, 0)


def _norm_glu_kernel(x_ref, g_ref, wa_ref, wb_ref, y_ref, u_ref):
    _norm_into(x_ref, g_ref, u_ref)
    u = u_ref[...]
    gate = _sigmoid(jnp.dot(u, wb_ref[...], preferred_element_type=F32))
    a = jnp.dot(u, wa_ref[...], preferred_element_type=F32)
    y_ref[...] = (a * gate).astype(y_ref.dtype)


def _norm_glu(x, g, w, first_a, first_b, tm, tn):
    S, D = x.shape
    return pl.pallas_call(
        _norm_glu_kernel,
        grid=(S // tm, D // tn),
        in_specs=[pl.BlockSpec((tm, D), lambda i, j: (i, 0)),
                  pl.BlockSpec((1, D), lambda i, j: (0, 0)),
                  pl.BlockSpec((D, tn), lambda i, j: (0, first_a + j)),
                  pl.BlockSpec((D, tn), lambda i, j: (0, first_b + j))],
        out_specs=[pl.BlockSpec((tm, tn), lambda i, j: (i, j)),
                   pl.BlockSpec((tm, D), lambda i, j: (i, 0))],
        out_shape=[jax.ShapeDtypeStruct((S, D), BF16), jax.ShapeDtypeStruct((S, D), BF16)],
        compiler_params=_params("arbitrary", "arbitrary"),
        name="norm_glu",
    )(x, g, w, w)


def _proj_scaled_kernel(u_ref, w_ref, cs_ref, o_ref):
    r = jnp.dot(u_ref[...], w_ref[...].astype(BF16), preferred_element_type=F32)
    o_ref[...] = (r * cs_ref[...]).astype(o_ref.dtype)


def _proj_scaled(u, w, col_scale, n_out, tm, tn):
    S, D = u.shape
    return pl.pallas_call(
        _proj_scaled_kernel,
        grid=(S // tm, n_out // tn),
        in_specs=[pl.BlockSpec((tm, D), lambda i, j: (i, 0)),
                  pl.BlockSpec((D, tn), lambda i, j: (0, j)),
                  pl.BlockSpec((1, tn), lambda i, j: (0, j))],
        out_specs=pl.BlockSpec((tm, tn), lambda i, j: (i, j)),
        out_shape=jax.ShapeDtypeStruct((S, n_out), BF16),
        compiler_params=_params("arbitrary", "arbitrary"),
        name="proj_qkv",
    )(u, w, col_scale)


def _proj_sigmoid_kernel(u_ref, w_ref, o_ref):
    o_ref[...] = jnp.dot(u_ref[...], w_ref[...].astype(BF16), preferred_element_type=F32).astype(o_ref.dtype)


def _proj_sigmoid(u, w, first_col_block, n_out, tm, tn):
    S, D = u.shape
    return pl.pallas_call(
        _proj_sigmoid_kernel,
        grid=(S // tm, n_out // tn),
        in_specs=[pl.BlockSpec((tm, D), lambda i, j: (i, 0)),
                  pl.BlockSpec((D, tn), lambda i, j: (0, first_col_block + j))],
        out_specs=pl.BlockSpec((tm, tn), lambda i, j: (i, j)),
        out_shape=jax.ShapeDtypeStruct((S, n_out), BF16),
        compiler_params=_params("arbitrary", "arbitrary"),
        name="proj_gates",
    )(u, w)


STAGE_UNROLL = 8


def _attn_kernel(slope_ref, q_ref, qn_ref, k_ref, v_ref, qaug_ref, kaug_ref, dx_ref, lq1_ref, lk1_ref, lq2_ref,
                 lk2_ref, g_ref, o_ref, m_ref, l_ref, acc_ref, sa_ref, xa_ref, sb_ref, xb_ref, sc_ref, xc_ref, *, t):
    w = 2 * t
    h = pl.program_id(0)
    i = pl.program_id(1)
    last_pair = i // 2
    tile_is_odd = i % 2 == 1
    slope = slope_ref[h]
    shift = slope * w
    reps = w // LANES

    q_extra = jnp.broadcast_to(qaug_ref[...], (t, HEAD_DIM)).astype(BF16)

    def augmented(ref):
        return tuple(jnp.concatenate([ref[:, j * HEAD_DIM:(j + 1) * HEAD_DIM], q_extra], axis=1) for j in range(2))

    q = augmented(q_ref)

    def pair_rows(pair):
        return pl.ds(pl.multiple_of(pair * w, w), w)

    def scores(pair, buf, q_maps=q):
        s_ref, x_ref = buf
        rows = pair_rows(pair)
        for j in range(2):
            k_j = jnp.concatenate([k_ref[rows, j * HEAD_DIM:(j + 1) * HEAD_DIM], kaug_ref[...]], axis=1)
            s = lax.dot_general(q_maps[j], k_j, (((1,), (1,)), ((), ())), preferred_element_type=F32)
            s_ref[j] = s
            lane_max = s[:, :LANES]
            for r in range(1, reps):
                lane_max = jnp.maximum(lane_max, s[:, r * LANES:(r + 1) * LANES])
            x_ref[j] = lane_max

    FULL, PAIR_DIAG, SINGLE_DIAG = range(3)

    def softmax_pv(pair, buf, kind, first=False):
        s_ref, x_ref = buf
        width = t if kind == SINGLE_DIAG else w
        reps = width // LANES
        ps, alphas = [], []
        for j in range(2):
            if kind == FULL:
                s = s_ref[j]
                m_cur = jnp.max(x_ref[j], axis=1, keepdims=True)
            else:
                bias = slope * dx_ref[...]
                if kind == PAIR_DIAG:
                    s = jnp.concatenate([s_ref[j, :, :t], s_ref[j, :, t:] + bias], axis=1)
                else:
                    s = s_ref[j, :, :t] + bias
                m_cur = jnp.max(s, axis=1, keepdims=True)
            if first:
                m_new = jnp.broadcast_to(m_cur, (t, LANES))
            else:
                m_prev = m_ref[j] - shift
                m_new = jnp.maximum(m_prev, m_cur)
                alpha = jnp.exp2(m_prev - m_new)
                alphas.append(jnp.concatenate([alpha] * (HEAD_WIDTH // LANES), axis=1))
            p = jnp.exp2(s - jnp.concatenate([m_new] * reps, axis=1))
            lane_sums = p[:, :LANES]
            for r in range(1, reps):
                lane_sums = lane_sums + p[:, r * LANES:(r + 1) * LANES]
            l_ref[j] = lane_sums if first else alpha * l_ref[j] + lane_sums
            m_ref[j] = m_new
            ps.append(p.astype(BF16))
        v_rows = pl.ds(pl.multiple_of(pair * w, w), width)
        pv = jnp.dot(jnp.concatenate(ps, axis=0), v_ref[v_rows, :], preferred_element_type=F32)
        acc_ref[...] = pv if first else acc_ref[...] * jnp.concatenate(alphas, axis=0) + pv

    buf_a = (sa_ref, xa_ref)
    buf_b = (sb_ref, xb_ref)
    buf_c = (sc_ref, xc_ref)

    def stage(pair, cur, nxt):
        scores(pair + 1, nxt)
        softmax_pv(pair, cur, FULL)

    def stages(first_pair, count):
        for d in range(0, count, 2):
            stage(first_pair + d, buf_a, buf_b)
            stage(first_pair + d + 1, buf_b, buf_a)

    def diagonal_and_next(buf, first=False):
        for parity, kind in ((True, PAIR_DIAG), (False, SINGLE_DIAG)):
            @pl.when(tile_is_odd == parity)
            def _(kind=kind):
                softmax_pv(last_pair, buf, kind, first=first)
                scores(0, buf_c, augmented(qn_ref))

    @pl.when(i == 0)
    def _():
        scores(0, buf_c)

    @pl.when(last_pair == 0)
    def _():
        diagonal_and_next(buf_c, first=True)

    @pl.when(last_pair > 0)
    def _():
        scores(1, buf_a)
        softmax_pv(0, buf_c, FULL, first=True)

    rest = jnp.maximum(last_pair - 1, 0)

    def unrolled(u, c):
        stages(1 + STAGE_UNROLL * u, STAGE_UNROLL)
        return c

    lax.fori_loop(0, rest // STAGE_UNROLL, unrolled, 0)
    piece = STAGE_UNROLL // 2
    while piece >= 2:
        @pl.when((rest & piece) != 0)
        def _(piece=piece):
            stages(1 + (rest & ~(2 * piece - 1)), piece)

        piece //= 2

    odd = rest % 2 == 1

    @pl.when(jnp.logical_and(last_pair > 0, odd))
    def _():
        stage(last_pair - 1, buf_a, buf_b)
        diagonal_and_next(buf_b)

    @pl.when(jnp.logical_and(last_pair > 0, jnp.logical_not(odd)))
    def _():
        diagonal_and_next(buf_a)

    lam = (jnp.exp(jnp.sum(lq1_ref[...] * lk1_ref[...], axis=1, keepdims=True))
           - jnp.exp(jnp.sum(lq2_ref[...] * lk2_ref[...], axis=1, keepdims=True)) + LAMBDA_INIT)
    o1 = acc_ref[:t, :] / jnp.sum(l_ref[0], axis=1, keepdims=True)
    o2 = acc_ref[t:, :] / jnp.sum(l_ref[1], axis=1, keepdims=True)
    o = o1 - lam * o2
    o_ref[...] = (_rms_rows(o, g_ref[...]) * (1.0 - LAMBDA_INIT)).astype(o_ref.dtype)


BIAS_SPLIT = 256
LOG2E_PIECES = 3


def _diag_extra(t):
    r = lax.broadcasted_iota(jnp.int32, (t, t), 0)
    c = lax.broadcasted_iota(jnp.int32, (t, t), 1)
    allowed = (c // CHUNK) <= (r // CHUNK)
    return jnp.where(allowed, (-2 * jnp.maximum(c - r, 0)).astype(F32), -jnp.inf)


def _bias_columns(w):
    pieces, rest = [], LOG2E
    for _ in range(LOG2E_PIECES):
        p = float(np.asarray(rest, np.float32).astype(BF16).astype(np.float32))
        pieces.append(p)
        rest -= p
    q_cols = jnp.zeros((1, HEAD_DIM), F32).at[0, :2 * LOG2E_PIECES].set(jnp.asarray(pieces * 2, F32))
    slopes = jnp.exp2(-(jnp.arange(N_HEADS, dtype=F32) + 1.0) * (8.0 / N_HEADS))
    c = jnp.arange(w, dtype=jnp.int32)
    lo = (c % BIAS_SPLIT).astype(F32)
    hi = (c - c % BIAS_SPLIT).astype(F32)
    k_cols = jnp.concatenate([jnp.tile(lo[:, None], (1, LOG2E_PIECES)), jnp.tile(hi[:, None], (1, LOG2E_PIECES)),
                              jnp.zeros((w, HEAD_DIM - 2 * LOG2E_PIECES), F32)], axis=1)
    k_cols = (slopes[:, None, None] * k_cols[None]).astype(BF16)
    return q_cols, k_cols, slopes * LOG2E


def _attention(qkv, lq1, lk1, lq2, lk2, subln_g, t):
    S = qkv.shape[0]
    n_tiles = S // t
    w = 2 * t
    assert n_tiles % 2 == 0
    q_cols, k_cols, slopes = _bias_columns(w)
    vec = pl.BlockSpec((1, HEAD_DIM), lambda h, i: (0, 0))
    return pl.pallas_call(
        functools.partial(_attn_kernel, t=t),
        grid=(N_HEADS, n_tiles),
        in_specs=[pl.BlockSpec(memory_space=pltpu.SMEM),
                  pl.BlockSpec((t, HEAD_WIDTH), lambda h, i: (i, h)),
                  pl.BlockSpec((t, HEAD_WIDTH), lambda h, i: (jnp.minimum(i + 1, n_tiles - 1), h)),
                  pl.BlockSpec((S, HEAD_WIDTH), lambda h, i: (0, N_HEADS + h)),
                  pl.BlockSpec((S, HEAD_WIDTH), lambda h, i: (0, 2 * N_HEADS + h)),
                  vec,
                  pl.BlockSpec((None, w, HEAD_DIM), lambda h, i: (h, 0, 0)),
                  pl.BlockSpec((t, t), lambda h, i: (0, 0), pipeline_mode=pl.Buffered(1)),
                  vec, vec, vec, vec,
                  pl.BlockSpec((1, HEAD_WIDTH), lambda h, i: (0, 0))],
        out_specs=pl.BlockSpec((t, HEAD_WIDTH), lambda h, i: (i, h)),
        out_shape=jax.ShapeDtypeStruct((S, N_HEADS * HEAD_WIDTH), BF16),
        scratch_shapes=[pltpu.VMEM((2, t, LANES), F32),
                        pltpu.VMEM((2, t, LANES), F32),
                        pltpu.VMEM((2 * t, HEAD_WIDTH), F32)] + 3 * [
                            pltpu.VMEM((2, t, w), F32),
                            pltpu.VMEM((2, t, LANES), F32)],
        compiler_params=_params("arbitrary", "arbitrary"),
        name="attention",
    )(slopes, qkv, qkv, qkv, qkv, q_cols, k_cols, _diag_extra(t), lq1, lk1, lq2, lk2, subln_g)


CONV_ROWS = 128
CONV_COLS = 256
SUBLANES = 8


def _conv_kernel(yp_ref, y_ref, w_ref, b_ref, lg_ref, lb_ref, z_ref, ext_ref, xs_ref, cv_ref):
    tm = y_ref.shape[0]
    cb = pl.program_id(1)
    n_cb = pl.num_programs(1)
    first = pl.program_id(0) == 0
    ext_ref[:CONV_HALO, :] = jnp.where(first, 0.0, yp_ref[...].astype(F32))
    ext_ref[CONV_HALO:, :] = y_ref[...].astype(F32)
    shifted_rows = xs_ref.shape[1]
    for r in range(1, SUBLANES):
        xs_ref[r - 1] = ext_ref[r:r + shifted_rows, :]
    lead = CONV_HALO - (CONV_KERNEL - 1)
    for r0 in range(0, tm, CONV_ROWS):
        acc = jnp.broadcast_to(b_ref[...], (CONV_ROWS, CONV_COLS))
        for j in range(CONV_KERNEL):
            r, a = (lead + j) % SUBLANES, (lead + j) // SUBLANES
            src = ext_ref if r == 0 else xs_ref.at[r - 1]
            lo = r0 + SUBLANES * a
            acc = acc + src[lo:lo + CONV_ROWS, :] * w_ref[j:j + 1, :]
        cv_ref[cb, r0:r0 + CONV_ROWS, :] = acc

    @pl.when(cb == n_cb - 1)
    def _():
        n_blocks = cv_ref.shape[0]
        n_ch = n_blocks * CONV_COLS

        def rows_body(rc, c):
            rows = pl.ds(pl.multiple_of(rc * CONV_ROWS, CONV_ROWS), CONV_ROWS)
            blocks = [cv_ref[k, rows, :] for k in range(n_blocks)]
            mu = sum(jnp.sum(b, axis=1, keepdims=True) for b in blocks) / n_ch
            var = sum(jnp.sum((b - mu) * (b - mu), axis=1, keepdims=True) for b in blocks) / n_ch
            inv = lax.rsqrt(var + NORM_EPS)
            for k, b in enumerate(blocks):
                cols = slice(k * CONV_COLS, (k + 1) * CONV_COLS)
                zn = (b - mu) * inv * lg_ref[:, cols] + lb_ref[:, cols]
                z_ref[rows, cols] = (zn * _sigmoid(zn)).astype(z_ref.dtype)
            return c

        lax.fori_loop(0, tm // CONV_ROWS, rows_body, 0)


def _conv_branch(y, dw_w, dw_b, ln_g, ln_b, tm):
    S, C = y.shape
    halo_blocks = tm // CONV_HALO
    n_cb = C // CONV_COLS
    row = pl.BlockSpec((1, C), lambda i, c: (0, 0))
    return pl.pallas_call(
        _conv_kernel,
        grid=(S // tm, n_cb),
        in_specs=[pl.BlockSpec((CONV_HALO, CONV_COLS), lambda i, c: (jnp.maximum(i * halo_blocks - 1, 0), c)),
                  pl.BlockSpec((tm, CONV_COLS), lambda i, c: (i, c)),
                  pl.BlockSpec((CONV_KERNEL, CONV_COLS), lambda i, c: (0, c)),
                  pl.BlockSpec((1, CONV_COLS), lambda i, c: (0, c)),
                  row, row],
        out_specs=pl.BlockSpec((tm, C), lambda i, c: (i, 0)),
        out_shape=jax.ShapeDtypeStruct((S, C), BF16),
        scratch_shapes=[pltpu.VMEM((tm + CONV_HALO, CONV_COLS), F32),
                        pltpu.VMEM((SUBLANES - 1, tm + CONV_HALO - SUBLANES, CONV_COLS), F32),
                        pltpu.VMEM((n_cb, tm, CONV_COLS), F32)],
        compiler_params=_params("arbitrary", "arbitrary"),
        name="conv_branch",
    )(y, y, dw_w, dw_b, ln_g, ln_b)


def _merge_kernel(o_ref, z_ref, wa_ref, wc_ref, bc_ref, ga_ref, gc_ref, out_ref):
    a = jnp.dot(o_ref[...], wa_ref[...], preferred_element_type=F32)
    c = jnp.dot(z_ref[...], wc_ref[...], preferred_element_type=F32) + bc_ref[...]
    out_ref[...] = (_sigmoid(ga_ref[...].astype(F32)) * a + _sigmoid(gc_ref[...].astype(F32)) * c).astype(out_ref.dtype)


def _merge(o, z, wa, wc, bc, gates, tm, tn):
    S, D = o.shape
    nb = D // tn
    return pl.pallas_call(
        _merge_kernel,
        grid=(S // tm, nb),
        in_specs=[pl.BlockSpec((tm, D), lambda i, j: (i, 0)),
                  pl.BlockSpec((tm, D), lambda i, j: (i, 0)),
                  pl.BlockSpec((D, tn), lambda i, j: (0, j)),
                  pl.BlockSpec((D, tn), lambda i, j: (0, j)),
                  pl.BlockSpec((1, tn), lambda i, j: (0, j)),
                  pl.BlockSpec((tm, tn), lambda i, j: (i, j)),
                  pl.BlockSpec((tm, tn), lambda i, j: (i, nb + j))],
        out_specs=pl.BlockSpec((tm, tn), lambda i, j: (i, j)),
        out_shape=jax.ShapeDtypeStruct((S, D), BF16),
        compiler_params=_params("arbitrary", "arbitrary"),
        name="merge",
    )(o, z, wa, wc, bc, gates, gates)


def _mixout_kernel(m_ref, w_ref, x_ref, g_ref, h_ref, f_ref):
    h = x_ref[...] + jnp.dot(m_ref[...], w_ref[...], preferred_element_type=F32)
    h_ref[...] = h
    f_ref[...] = _rms_rows(h, g_ref[...]).astype(f_ref.dtype)


def _mixout(mixed, w, x, g, tm):
    S, D = x.shape
    tile = pl.BlockSpec((tm, D), lambda i: (i, 0))
    return pl.pallas_call(
        _mixout_kernel,
        grid=(S // tm,),
        in_specs=[tile,
                  pl.BlockSpec((D, D), lambda i: (0, 0)),
                  tile,
                  pl.BlockSpec((1, D), lambda i: (0, 0))],
        out_specs=[tile, tile],
        out_shape=[jax.ShapeDtypeStruct((S, D), F32), jax.ShapeDtypeStruct((S, D), BF16)],
        compiler_params=_params("arbitrary"),
        name="mixout",
    )(mixed, w, x, g)


def _ffn_in_kernel(f_ref, wg_ref, wu_ref, a_ref):
    f = f_ref[...]
    g = jnp.dot(f, wg_ref[...].astype(BF16), preferred_element_type=F32)
    u = jnp.dot(f, wu_ref[...].astype(BF16), preferred_element_type=F32)
    a_ref[...] = (g * _sigmoid(g) * u).astype(a_ref.dtype)


def _ffn_in(f, w, tm, tn):
    S, D = f.shape
    d_ff = w.shape[1] // 2
    nb = d_ff // tn
    return pl.pallas_call(
        _ffn_in_kernel,
        grid=(S // tm, nb),
        in_specs=[pl.BlockSpec((tm, D), lambda i, j: (i, 0)),
                  pl.BlockSpec((D, tn), lambda i, j: (0, j)),
                  pl.BlockSpec((D, tn), lambda i, j: (0, nb + j))],
        out_specs=pl.BlockSpec((tm, tn), lambda i, j: (i, j)),
        out_shape=jax.ShapeDtypeStruct((S, d_ff), BF16),
        compiler_params=_params("arbitrary", "arbitrary"),
        name="ffn_in",
    )(f, w, w)


def _ffn_out_kernel(a_ref, w_ref, h_ref, g_ref, out_ref):
    k = pl.program_id(1)
    last = pl.num_programs(1) - 1

    def partial_product():
        return jnp.dot(a_ref[...], w_ref[...], preferred_element_type=F32)

    @pl.when(k == 0)
    def _():
        out_ref[...] = h_ref[...] + partial_product()

    @pl.when(jnp.logical_and(k > 0, k < last))
    def _():
        out_ref[...] += partial_product()

    @pl.when(k == last)
    def _():
        out_ref[...] = _rms_rows(out_ref[...] + partial_product(), g_ref[...])


def _ffn_out(a, w, h, g, tm, tk):
    S, D = h.shape
    d_ff = a.shape[1]
    assert d_ff // tk >= 2
    return pl.pallas_call(
        _ffn_out_kernel,
        grid=(S // tm, d_ff // tk),
        in_specs=[pl.BlockSpec((tm, tk), lambda i, k: (i, k)),
                  pl.BlockSpec((tk, D), lambda i, k: (k, 0)),
                  pl.BlockSpec((tm, D), lambda i, k: (i, 0)),
                  pl.BlockSpec((1, D), lambda i, k: (0, 0))],
        out_specs=pl.BlockSpec((tm, D), lambda i, k: (i, 0)),
        out_shape=jax.ShapeDtypeStruct((S, D), F32),
        compiler_params=_params("arbitrary", "arbitrary"),
        name="ffn_out",
    )(a, w, h, g)


def _tiles(S, d_ff):
    return dict(
        proj_rows=min(S, 1024), proj_cols=1024, glu_cols=512,
        attn=min(S, 512),
        conv_rows=min(S, 512),
        merge_rows=min(S, 1024), merge_cols=1024,
        mix_rows=min(S, 512),
        ffn_rows=min(S, 1024), ffn_cols=512,
        out_rows=min(S, 1024), out_k=d_ff // 4,
    )


def kernel(x, norm_mix_g, w_in, lambda_q1, lambda_k1, lambda_q2, lambda_k2, subln_g, w_attn_out, dw_w, dw_b,
           conv_ln_g, conv_ln_b, w_conv_out, b_conv_out, w_mix_out, norm_ffn_g, w_ffn_in, w_ffn_out, norm_final_g):
    B, S, D = x.shape
    depth = w_in.shape[0]
    assert B == 1 and depth == 1 and D == N_HEADS * HEAD_WIDTH
    d_ff = w_ffn_out.shape[1]
    t = _tiles(S, d_ff)
    row = lambda v: v.reshape(1, -1).astype(F32)

    h0 = x.reshape(S, D)
    w_in_f = w_in[0]

    q_scale = HEAD_DIM ** -0.5 * LOG2E
    col_scale = jnp.concatenate([jnp.full((1, D), q_scale, F32), jnp.ones((1, 2 * D), F32)], axis=1)
    pc = t["proj_cols"]
    gc = t["glu_cols"]
    w_glu = w_in_f[:, 3 * D:5 * D].astype(BF16)
    y, u = _norm_glu(h0, row(norm_mix_g[0]), w_glu, 0, D // gc, t["proj_rows"], gc)
    qkv = _proj_scaled(u, w_in_f, col_scale, 3 * D, t["proj_rows"], pc)
    gates = _proj_sigmoid(u, w_in_f, 5 * D // pc, 2 * D, t["proj_rows"], pc)

    o = _attention(qkv, row(lambda_q1[0]), row(lambda_k1[0]), row(lambda_q2[0]), row(lambda_k2[0]),
                   row(subln_g[0]), t["attn"])
    z = _conv_branch(y, dw_w[0].astype(F32), row(dw_b[0]), row(conv_ln_g[0]), row(conv_ln_b[0]), t["conv_rows"])

    mixed = _merge(o, z, w_attn_out[0].astype(BF16), w_conv_out[0].astype(BF16), row(b_conv_out[0]), gates,
                   t["merge_rows"], t["merge_cols"])
    h1, f = _mixout(mixed, w_mix_out[0].astype(BF16), h0, row(norm_ffn_g[0]), t["mix_rows"])

    a = _ffn_in(f, w_ffn_in[0], t["ffn_rows"], t["ffn_cols"])
    out = _ffn_out(a, w_ffn_out[0].astype(BF16), h1, row(norm_final_g), t["out_rows"], t["out_k"])
    return out.reshape(B, S, D)
```
